```python
import math
import jax, jax.numpy as jnp
from jax import lax
import numpy as np

D_MODEL = 2048
BATCH = 2
SEQ = 8192
DEPTH = 4

N_MIXERS = 4
ROPE_THETA = 500000.0
ROT_FRACTION = 4
NORM_EPS = 1e-6
NEG_INF = -1e30
Q_BLOCK = 128

MOBA_HEADS = D_MODEL // 128
MOBA_HEAD_DIM = 128
MOBA_BLOCK = 256
MOBA_TOPK = 3
MOBA_Q_CHUNK = 32

MLA_HEADS = D_MODEL // 128
MLA_Q_RANK = D_MODEL // 4
MLA_KV_RANK = D_MODEL // 4
MLA_NOPE_DIM = 128
MLA_ROPE_DIM = 64
MLA_V_DIM = 128

SWA_HEADS = D_MODEL // 64
SWA_KV_HEADS = SWA_HEADS // 8
SWA_HEAD_DIM = 64
SWA_WINDOW = 128

DIFF_HEADS = D_MODEL // 256
DIFF_HEAD_DIM = 128

D_FF = ((8 * D_MODEL // 3 + 255) // 256) * 256
CONV_WIDTH = 3

kernel_name = 'hybrid_moba_mla_swa_diff_convffn'

F32 = jnp.float32


def _n_layers_of(m):
    return (DEPTH - m + N_MIXERS - 1) // N_MIXERS


def rms_norm(x, g):
    xf = x.astype(F32)
    y = xf * lax.rsqrt(jnp.mean(xf * xf, axis=-1, keepdims=True) + NORM_EPS)
    return (y * g.astype(F32)).astype(x.dtype)


def rope(x, pos, rot_dim):
    half = rot_dim // 2
    inv = ROPE_THETA ** (-jnp.arange(half, dtype=F32) / half)
    ang = pos.astype(F32)[:, None] * inv[None, :]
    cos = jnp.cos(ang)[None, :, None, :]
    sin = jnp.sin(ang)[None, :, None, :]
    xr = x[..., :rot_dim].astype(F32)
    x1, x2 = xr[..., :half], xr[..., half:]
    rot = jnp.concatenate([x1 * cos - x2 * sin, x2 * cos + x1 * sin], axis=-1).astype(x.dtype)
    return jnp.concatenate([rot, x[..., rot_dim:]], axis=-1)


def sweep_query_blocks(fn, arrays, block):
    B, S = arrays[0].shape[:2]
    nq = S // block
    blocks = tuple(jnp.moveaxis(a.reshape((B, nq, block) + a.shape[2:]), 1, 0) for a in arrays)
    out = lax.map(lambda xs: fn(xs[0], *xs[1]), (jnp.arange(nq), blocks))
    return jnp.moveaxis(out, 0, 1).reshape((B, S) + out.shape[3:])


def moba_attention(h, pos, wq, wk, wv, gq, gk, wo):
    B, S, _ = h.shape
    H, dh, BLK, QC = MOBA_HEADS, MOBA_HEAD_DIM, MOBA_BLOCK, MOBA_Q_CHUNK
    rot = dh // ROT_FRACTION
    q = rope(rms_norm((h @ wq).reshape(B, S, H, dh), gq), pos, rot)
    k = rope(rms_norm((h @ wk).reshape(B, S, H, dh), gk), pos, rot)
    v = (h @ wv).reshape(B, S, H, dh)
    pad = (-S) % BLK
    q, k, v = (jnp.pad(a, ((0, 0), (0, pad), (0, 0), (0, 0))) for a in (q, k, v))
    nb = (S + pad) // BLK
    topk = min(MOBA_TOPK, nb)
    kb = k.reshape(B, nb, BLK, H, dh).transpose(0, 3, 1, 2, 4)
    vb = v.reshape(B, nb, BLK, H, dh).transpose(0, 3, 1, 2, 4)
    k_mean = jnp.mean(kb.astype(F32), axis=3).astype(k.dtype)
    scale = dh ** -0.5
    b_idx = jnp.arange(B)[:, None, None, None]
    h_idx = jnp.arange(H)[None, :, None, None]
    blk_ids = jnp.arange(nb)
    key_off = jnp.arange(BLK)

    def chunk(c, qc):
        q_pos = c * QC + jnp.arange(QC)
        own = (c * QC) // BLK
        gate = jnp.einsum('bqhd,bhnd->bhqn', qc, k_mean).astype(F32)
        gate = jnp.where(blk_ids < own, gate, NEG_INF)
        _, idx = lax.top_k(gate, topk)
        valid = idx < own
        k_sel = kb[b_idx, h_idx, idx]
        v_sel = vb[b_idx, h_idx, idx]
        s_sel = jnp.einsum('bqhd,bhqnkd->bhqnk', qc, k_sel).astype(F32) * scale
        s_sel = jnp.where(valid[..., None], s_sel, NEG_INF).reshape(B, H, QC, topk * BLK)
        k_own = lax.dynamic_index_in_dim(kb, own, axis=2, keepdims=False)
        v_own = lax.dynamic_index_in_dim(vb, own, axis=2, keepdims=False)
        s_own = jnp.einsum('bqhd,bhkd->bhqk', qc, k_own).astype(F32) * scale
        s_own = jnp.where(own * BLK + key_off[None, :] <= q_pos[:, None], s_own, NEG_INF)
        p = jax.nn.softmax(jnp.concatenate([s_sel, s_own], axis=-1), axis=-1).astype(v.dtype)
        p_sel = p[..., :topk * BLK].reshape(B, H, QC, topk, BLK)
        p_own = p[..., topk * BLK:]
        return (jnp.einsum('bhqnk,bhqnkd->bqhd', p_sel, v_sel)
                + jnp.einsum('bhqk,bhkd->bqhd', p_own, v_own))

    o = sweep_query_blocks(chunk, (q,), QC)[:, :S]
    return o.reshape(B, S, H * dh) @ wo


def mla_attention(h, pos, wq_a, g_qa, wq_b, wkv_a, g_kva, wkv_b, g_qn, g_kn, g_qr, g_kr, wo):
    B, S, _ = h.shape
    H, NOPE, RD, VD = MLA_HEADS, MLA_NOPE_DIM, MLA_ROPE_DIM, MLA_V_DIM
    cq = rms_norm(h @ wq_a, g_qa)
    q = (cq @ wq_b).reshape(B, S, H, NOPE + RD)
    q_nope = rms_norm(q[..., :NOPE], g_qn)
    q_rope = rope(rms_norm(q[..., NOPE:], g_qr), pos, RD)
    kv_a = h @ wkv_a
    ckv = rms_norm(kv_a[..., :MLA_KV_RANK], g_kva)
    k_rope = rope(rms_norm(kv_a[..., MLA_KV_RANK:], g_kr)[:, :, None, :], pos, RD)[:, :, 0]
    kv = (ckv @ wkv_b).reshape(B, S, H, NOPE + VD)
    k_nope = rms_norm(kv[..., :NOPE], g_kn)
    v = kv[..., NOPE:]
    scale = (NOPE + RD) ** -0.5

    def block(i, qn, qr):
        q_pos = i * Q_BLOCK + jnp.arange(Q_BLOCK)
        s = (jnp.einsum('bqhd,bkhd->bhqk', qn, k_nope)
             + jnp.einsum('bqhr,bkr->bhqk', qr, k_rope)).astype(F32) * scale
        s = jnp.where(pos[None, :] <= q_pos[:, None], s, NEG_INF)
        p = jax.nn.softmax(s, axis=-1).astype(v.dtype)
        return jnp.einsum('bhqk,bkhd->bqhd', p, v)

    o = sweep_query_blocks(block, (q_nope, q_rope), Q_BLOCK)
    return o.reshape(B, S, H * VD) @ wo


def swa_attention(h, pos, wq, wk, wv, gq, gk, sinks, wo):
    B, S, _ = h.shape
    HQ, HKV, dh, W = SWA_HEADS, SWA_KV_HEADS, SWA_HEAD_DIM, SWA_WINDOW
    G = HQ // HKV
    rot = dh // ROT_FRACTION
    q = rope(rms_norm((h @ wq).reshape(B, S, HQ, dh), gq), pos, rot)
    k = rope(rms_norm((h @ wk).reshape(B, S, HKV, dh), gk), pos, rot)
    v = (h @ wv).reshape(B, S, HKV, dh)
    nb = S // W
    qb = q.reshape(B, nb, W, HKV, G, dh)

    def with_prev(a):
        ab = a.reshape(B, nb, W, HKV, dh)
        prev = jnp.pad(ab, ((0, 0), (1, 0), (0, 0), (0, 0), (0, 0)))[:, :nb]
        return jnp.concatenate([prev, ab], axis=2)

    kk, vv = with_prev(k), with_prev(v)
    s = jnp.einsum('bnqkgd,bnjkd->bnkgqj', qb, kk).astype(F32) * dh ** -0.5
    qi = jnp.arange(W)[:, None]
    kj = jnp.arange(2 * W)[None, :]
    dist = qi + W - kj
    band = (dist >= 0) & (dist < W)
    valid = band[None] & ((kj >= W)[None] | (jnp.arange(nb) > 0)[:, None, None])
    s = jnp.where(valid[None, :, None, None], s, NEG_INF)
    sink = jnp.broadcast_to(sinks.astype(F32).reshape(1, 1, HKV, G, 1, 1), s.shape[:-1] + (1,))
    p = jax.nn.softmax(jnp.concatenate([s, sink], axis=-1), axis=-1)[..., :2 * W].astype(v.dtype)
    o = jnp.einsum('bnkgqj,bnjkd->bnqkgd', p, vv).reshape(B, S, HQ * dh)
    return o @ wo


def diff_attention(h, pos, wq, wk, wv, gq, gk, lq1, lk1, lq2, lk2, g_sub, wo, lambda_init):
    B, S, _ = h.shape
    H, dh = DIFF_HEADS, DIFF_HEAD_DIM
    rot = dh // ROT_FRACTION

    def qk(w, g):
        a = rms_norm((h @ w).reshape(B, S, H * 2, dh), g)
        return rope(a, pos, rot).reshape(B, S, H, 2, dh)

    q = qk(wq, gq)
    k = qk(wk, gk)
    v = (h @ wv).reshape(B, S, H, 2 * dh)
    lam = (jnp.exp(jnp.sum(lq1.astype(F32) * lk1.astype(F32)))
           - jnp.exp(jnp.sum(lq2.astype(F32) * lk2.astype(F32))) + lambda_init)
    scale = dh ** -0.5

    def block(i, qb):
        q_pos = i * Q_BLOCK + jnp.arange(Q_BLOCK)
        s = jnp.einsum('bqhcd,bkhcd->bhcqk', qb, k).astype(F32) * scale
        s = jnp.where(pos[None, :] <= q_pos[:, None], s, NEG_INF)
        p = jax.nn.softmax(s, axis=-1)
        a = (p[:, :, 0] - lam * p[:, :, 1]).astype(v.dtype)
        return jnp.einsum('bhqk,bkhe->bqhe', a, v)

    o = sweep_query_blocks(block, (q,), Q_BLOCK)
    o = rms_norm(o, g_sub) * (1.0 - lambda_init)
    return o.reshape(B, S, H * 2 * dh) @ wo


def conv_glu_ffn(h, w_gate, w_up, conv_w, conv_b, w_down):
    g = h @ w_gate
    g = lax.conv_general_dilated(
        g, conv_w[:, None, :].astype(g.dtype), window_strides=(1,),
        padding=[(CONV_WIDTH - 1, 0)], dimension_numbers=('NWC', 'WIO', 'NWC'),
        feature_group_count=g.shape[-1]) + conv_b.astype(g.dtype)
    return (jax.nn.silu(g) * (h @ w_up)) @ w_down


def setup_inputs(seed: int = 0) -> dict:
    key = jax.random.key(seed)
    ks = iter(jax.random.split(key, 64))
    D = D_MODEL
    nA, nB, nC, nD = (_n_layers_of(m) for m in range(N_MIXERS))
    out_s = (2 * DEPTH) ** -0.5

    def w(shape, fan_in, scale=1.0):
        return jax.random.normal(next(ks), shape, F32) * (scale * fan_in ** -0.5)

    def gain(shape):
        return 1.0 + 0.02 * jax.random.normal(next(ks), shape, F32)

    def small(shape, s):
        return s * jax.random.normal(next(ks), shape, F32)

    inp = {}
    inp['x'] = jax.random.normal(next(ks), (BATCH, SEQ, D), F32)
    inp['attn_norm'] = gain((DEPTH, D))
    inp['ffn_norm'] = gain((DEPTH, D))
    ad = MOBA_HEADS * MOBA_HEAD_DIM
    inp['moba_wq'] = w((nA, D, ad), D)
    inp['moba_wk'] = w((nA, D, ad), D)
    inp['moba_wv'] = w((nA, D, ad), D)
    inp['moba_gq'] = gain((nA, MOBA_HEAD_DIM))
    inp['moba_gk'] = gain((nA, MOBA_HEAD_DIM))
    inp['moba_wo'] = w((nA, ad, D), ad, out_s)
    inp['mla_wq_a'] = w((nB, D, MLA_Q_RANK), D)
    inp['mla_g_qa'] = gain((nB, MLA_Q_RANK))
    inp['mla_wq_b'] = w((nB, MLA_Q_RANK, MLA_HEADS * (MLA_NOPE_DIM + MLA_ROPE_DIM)), MLA_Q_RANK)
    inp['mla_wkv_a'] = w((nB, D, MLA_KV_RANK + MLA_ROPE_DIM), D)
    inp['mla_g_kva'] = gain((nB, MLA_KV_RANK))
    inp['mla_wkv_b'] = w((nB, MLA_KV_RANK, MLA_HEADS * (MLA_NOPE_DIM + MLA_V_DIM)), MLA_KV_RANK)
    inp['mla_g_qn'] = gain((nB, MLA_NOPE_DIM))
    inp['mla_g_kn'] = gain((nB, MLA_NOPE_DIM))
    inp['mla_g_qr'] = gain((nB, MLA_ROPE_DIM))
    inp['mla_g_kr'] = gain((nB, MLA_ROPE_DIM))
    inp['mla_wo'] = w((nB, MLA_HEADS * MLA_V_DIM, D), MLA_HEADS * MLA_V_DIM, out_s)
    inp['swa_wq'] = w((nC, D, SWA_HEADS * SWA_HEAD_DIM), D)
    inp['swa_wk'] = w((nC, D, SWA_KV_HEADS * SWA_HEAD_DIM), D)
    inp['swa_wv'] = w((nC, D, SWA_KV_HEADS * SWA_HEAD_DIM), D)
    inp['swa_gq'] = gain((nC, SWA_HEAD_DIM))
    inp['swa_gk'] = gain((nC, SWA_HEAD_DIM))
    inp['swa_sinks'] = small((nC, SWA_HEADS), 0.5)
    inp['swa_wo'] = w((nC, SWA_HEADS * SWA_HEAD_DIM, D), SWA_HEADS * SWA_HEAD_DIM, out_s)
    dq = DIFF_HEADS * 2 * DIFF_HEAD_DIM
    inp['diff_wq'] = w((nD, D, dq), D)
    inp['diff_wk'] = w((nD, D, dq), D)
    inp['diff_wv'] = w((nD, D, dq), D)
    inp['diff_gq'] = gain((nD, DIFF_HEAD_DIM))
    inp['diff_gk'] = gain((nD, DIFF_HEAD_DIM))
    inp['diff_lq1'] = small((nD, DIFF_HEAD_DIM), 0.1)
    inp['diff_lk1'] = small((nD, DIFF_HEAD_DIM), 0.1)
    inp['diff_lq2'] = small((nD, DIFF_HEAD_DIM), 0.1)
    inp['diff_lk2'] = small((nD, DIFF_HEAD_DIM), 0.1)
    inp['diff_g_sub'] = gain((nD, 2 * DIFF_HEAD_DIM))
    inp['diff_wo'] = w((nD, dq, D), dq, out_s)
    inp['ffn_w_gate'] = w((DEPTH, D, D_FF), D)
    inp['ffn_w_up'] = w((DEPTH, D, D_FF), D)
    inp['ffn_conv_w'] = w((DEPTH, CONV_WIDTH, D_FF), CONV_WIDTH)
    inp['ffn_conv_b'] = small((DEPTH, D_FF), 0.02)
    inp['ffn_w_down'] = w((DEPTH, D_FF, D), D_FF, out_s)
    return inp


def reference(x, attn_norm, ffn_norm,
              moba_wq, moba_wk, moba_wv, moba_gq, moba_gk, moba_wo,
              mla_wq_a, mla_g_qa, mla_wq_b, mla_wkv_a, mla_g_kva, mla_wkv_b,
              mla_g_qn, mla_g_kn, mla_g_qr, mla_g_kr, mla_wo,
              swa_wq, swa_wk, swa_wv, swa_gq, swa_gk, swa_sinks, swa_wo,
              diff_wq, diff_wk, diff_wv, diff_gq, diff_gk,
              diff_lq1, diff_lk1, diff_lq2, diff_lk2, diff_g_sub, diff_wo,
              ffn_w_gate, ffn_w_up, ffn_conv_w, ffn_conv_b, ffn_w_down):
    pos = jnp.arange(x.shape[1], dtype=jnp.int32)
    for i in range(DEPTH):
        m, j = i % N_MIXERS, i // N_MIXERS
        h = rms_norm(x, attn_norm[i])
        if m == 0:
            y = moba_attention(h, pos, moba_wq[j], moba_wk[j], moba_wv[j],
                               moba_gq[j], moba_gk[j], moba_wo[j])
        elif m == 1:
            y = mla_attention(h, pos, mla_wq_a[j], mla_g_qa[j], mla_wq_b[j], mla_wkv_a[j],
                              mla_g_kva[j], mla_wkv_b[j], mla_g_qn[j], mla_g_kn[j],
                              mla_g_qr[j], mla_g_kr[j], mla_wo[j])
        elif m == 2:
            y = swa_attention(h, pos, swa_wq[j], swa_wk[j], swa_wv[j], swa_gq[j],
                              swa_gk[j], swa_sinks[j], swa_wo[j])
        else:
            lambda_init = 0.8 - 0.6 * math.exp(-0.3 * i)
            y = diff_attention(h, pos, diff_wq[j], diff_wk[j], diff_wv[j], diff_gq[j],
                               diff_gk[j], diff_lq1[j], diff_lk1[j], diff_lq2[j],
                               diff_lk2[j], diff_g_sub[j], diff_wo[j], lambda_init)
        x = x + y
        h = rms_norm(x, ffn_norm[i])
        x = x + conv_glu_ffn(h, ffn_w_gate[i], ffn_w_up[i], ffn_conv_w[i],
                             ffn_conv_b[i], ffn_w_down[i])
    return x
```

```python
import functools
import math

import jax
import jax.numpy as jnp
from jax import lax
from jax.experimental import pallas as pl
from jax.experimental.pallas import tpu as pltpu

F32 = jnp.float32
BF16 = jnp.bfloat16

LANES = 128
NORM_EPS = 1e-6
NEG_INF = -1e30
ROPE_THETA = 500000.0
ROT_FRACTION = 4
DEPTH = 4

MOBA_HEAD_DIM = 128
MOBA_BLOCK = 256
MOBA_TOPK = 3

MLA_NOPE_DIM = 128
MLA_ROPE_DIM = 64
MLA_V_DIM = 128
MLA_QK_PAD = 256

SWA_HEAD_DIM = 64
SWA_GROUP = 8
SWA_WINDOW = 128

DIFF_HEAD_DIM = 128

CONV_WIDTH = 3
CONV_HALO = 8

VMEM_LIMIT = 56 * 1024 * 1024


def _cparams(*sem):
    return pltpu.CompilerParams(dimension_semantics=sem, vmem_limit_bytes=VMEM_LIMIT)


def _dot(a, b):
    return jnp.dot(a, b, preferred_element_type=F32)


def _dot_t(a, b):
    return lax.dot_general(a, b, (((1,), (1,)), ((), ())), preferred_element_type=F32)


def _rep(x, n):
    return x if n == 1 else jnp.concatenate([x] * n, axis=1)


def _rope_tables(seq, seg, rot):
    half = rot // 2
    inv = ROPE_THETA ** (-jnp.arange(half, dtype=F32) / half)
    ang = jnp.arange(seq, dtype=jnp.int32).astype(F32)[:, None] * inv[None, :]
    cos, sin = jnp.cos(ang), jnp.sin(ang)
    ones = jnp.ones((seq, seg - rot), F32)
    zeros_h = jnp.zeros((seq, half), F32)
    zeros_r = jnp.zeros((seq, seg - rot), F32)
    c = jnp.concatenate([cos, cos, ones], axis=1)
    s1 = jnp.concatenate([-sin, zeros_h, zeros_r], axis=1)
    s2 = jnp.concatenate([zeros_h, sin, zeros_r], axis=1)
    reps = LANES // seg
    return tuple(jnp.tile(t, (1, reps)) for t in (c, s1, s2))


def _proj_kernel(*refs, pattern, rope_half, out_scale, n_chunks, has_rope, has_hg, kmean_rows, mla_k):
    it = iter(refs)
    x_ref, g_ref, w_ref = next(it), next(it), next(it)
    hg_ref = next(it) if has_hg else None
    if has_rope:
        c_ref, s1_ref, s2_ref = next(it), next(it), next(it)
    kr_ref = next(it) if mla_k else None
    o_ref = next(it)
    km_ref = next(it) if kmean_rows else None
    xn_ref = next(it)

    @pl.when(pl.program_id(1) == 0)
    def _():
        x = x_ref[...]
        ms = jnp.mean(x * x, axis=-1, keepdims=True)
        xn_ref[...] = (x * lax.rsqrt(ms + NORM_EPS) * g_ref[...]).astype(BF16)

    acc = _dot(xn_ref[...], w_ref[...])
    for c in range(n_chunks):
        mode = pattern[c % len(pattern)]
        blk = acc[:, c * LANES:(c + 1) * LANES]
        if mode[0] == "norm":
            _, width, two, rope = mode
            sq = blk * blk
            if two:
                lo = lax.broadcasted_iota(jnp.int32, blk.shape, 1) < (LANES // 2)
                s_lo = jnp.sum(jnp.where(lo, sq, 0.0), axis=1, keepdims=True)
                s_hi = jnp.sum(jnp.where(lo, 0.0, sq), axis=1, keepdims=True)
                r = jnp.where(lo, lax.rsqrt(s_lo * (1.0 / width) + NORM_EPS),
                              lax.rsqrt(s_hi * (1.0 / width) + NORM_EPS))
            else:
                r = lax.rsqrt(jnp.sum(sq, axis=1, keepdims=True) * (1.0 / width) + NORM_EPS)
            blk = blk * r * hg_ref[:, c * LANES:(c + 1) * LANES]
            if rope:
                blk = (blk * c_ref[...]
                       + pltpu.roll(blk, LANES - rope_half, 1) * s1_ref[...]
                       + pltpu.roll(blk, rope_half, 1) * s2_ref[...])
            if out_scale != 1.0:
                blk = blk * out_scale
        if mla_k:
            o_ref[:, (2 * c) * LANES:(2 * c + 1) * LANES] = blk.astype(o_ref.dtype)
            o_ref[:, (2 * c + 1) * LANES:(2 * c + 2) * LANES] = kr_ref[...].astype(o_ref.dtype)
        else:
            o_ref[:, c * LANES:(c + 1) * LANES] = blk.astype(o_ref.dtype)
        if kmean_rows:
            for r_i in range(blk.shape[0] // kmean_rows):
                part = blk[r_i * kmean_rows:(r_i + 1) * kmean_rows, :]
                km_ref[r_i:r_i + 1, :, c * LANES:(c + 1) * LANES] = (
                    jnp.sum(part, axis=0, keepdims=True) * (1.0 / kmean_rows))[None]


def _norm_proj(x, gain, w, *, pattern, seq, xcol=0, hg=None, rope=None, rope_half=0, out_scale=1.0,
               out_dtype=None, kmean_rows=0, mla_kr=None, tm=512, tn=512):
    n_rows = x.shape[0]
    k_dim, n_out = w.shape
    tn = min(tn, n_out)
    out_dtype = out_dtype or BF16
    assert n_rows % tm == 0 and n_out % tn == 0 and seq % tm == 0
    n_chunks = tn // LANES
    assert n_chunks % len(pattern) == 0
    tiles_per_seq = seq // tm
    in_specs = [
        pl.BlockSpec((tm, k_dim), lambda i, j: (i, xcol)),
        pl.BlockSpec((1, k_dim), lambda i, j: (0, 0)),
        pl.BlockSpec((k_dim, tn), lambda i, j: (0, j)),
    ]
    args = [x, gain.reshape(1, k_dim).astype(F32), w]
    if hg is not None:
        in_specs.append(pl.BlockSpec((1, tn), lambda i, j: (0, j)))
        args.append(hg.reshape(1, n_out).astype(F32))
    if rope is not None:
        for t in rope:
            in_specs.append(pl.BlockSpec((tm, LANES), lambda i, j: (i % tiles_per_seq, 0)))
            args.append(t)
    out_mult = 1
    if mla_kr is not None:
        kr_arr, kr_col = mla_kr
        in_specs.append(pl.BlockSpec((tm, LANES), lambda i, j: (i, kr_col)))
        args.append(kr_arr)
        out_mult = 2
    out_shape = [jax.ShapeDtypeStruct((n_rows, n_out * out_mult), out_dtype)]
    out_specs = [pl.BlockSpec((tm, tn * out_mult), lambda i, j: (i, j))]
    if kmean_rows:
        out_shape.append(jax.ShapeDtypeStruct((n_rows // kmean_rows, 1, n_out), F32))
        out_specs.append(pl.BlockSpec((tm // kmean_rows, 1, tn), lambda i, j: (i, 0, j)))
    kern = functools.partial(
        _proj_kernel, pattern=tuple(pattern), rope_half=rope_half, out_scale=out_scale, n_chunks=n_chunks,
        has_rope=rope is not None, has_hg=hg is not None, kmean_rows=kmean_rows, mla_k=mla_kr is not None)
    res = pl.pallas_call(
        kern,
        grid=(n_rows // tm, n_out // tn),
        in_specs=in_specs,
        out_specs=out_specs,
        out_shape=out_shape,
        scratch_shapes=[pltpu.VMEM((tm, k_dim), BF16)],
        compiler_params=_cparams("parallel", "arbitrary"),
    )(*args)
    return res if kmean_rows else res[0]


def _mm_res_kernel(a_ref, w_ref, r_ref, o_ref):
    o_ref[...] = r_ref[...] + _dot(a_ref[...], w_ref[...])


def _matmul_residual(a, w, res, *, tm=512, tn=512):
    n_rows, k_dim = a.shape
    n_out = w.shape[1]
    assert n_rows % tm == 0 and n_out % tn == 0
    return pl.pallas_call(
        _mm_res_kernel,
        grid=(n_rows // tm, n_out // tn),
        in_specs=[
            pl.BlockSpec((tm, k_dim), lambda i, j: (i, 0)),
            pl.BlockSpec((k_dim, tn), lambda i, j: (0, j)),
            pl.BlockSpec((tm, tn), lambda i, j: (i, j)),
        ],
        out_specs=pl.BlockSpec((tm, tn), lambda i, j: (i, j)),
        out_shape=jax.ShapeDtypeStruct((n_rows, n_out), F32),
        compiler_params=_cparams("parallel", "parallel"),
    )(a, w, res)


def _ffn_up_kernel(x_ref, xh_ref, g_ref, wg_ref, wu_ref, cw_ref, cb_ref, o_ref, xn_ref, xhn_ref, gs_ref,
                   *, tm, tiles_per_seq):
    i = pl.program_id(0)

    @pl.when(pl.program_id(1) == 0)
    def _():
        x = x_ref[...]
        ms = jnp.mean(x * x, axis=-1, keepdims=True)
        xn_ref[...] = (x * lax.rsqrt(ms + NORM_EPS) * g_ref[...]).astype(BF16)
        xh = xh_ref[...]
        msh = jnp.mean(xh * xh, axis=-1, keepdims=True)
        xhn_ref[...] = (xh * lax.rsqrt(msh + NORM_EPS) * g_ref[...]).astype(BF16)

    wg = wg_ref[...]
    g = _dot(xn_ref[...], wg)
    gh = _dot(xhn_ref[...], wg)
    seq_start = (i % tiles_per_seq) == 0
    gs_ref[0:CONV_HALO, :] = jnp.where(seq_start, 0.0, gh)
    gs_ref[CONV_HALO:CONV_HALO + tm, :] = g
    y = (cw_ref[0:1, :] * gs_ref[CONV_HALO - 2:CONV_HALO - 2 + tm, :]
         + cw_ref[1:2, :] * gs_ref[CONV_HALO - 1:CONV_HALO - 1 + tm, :]
         + cw_ref[2:3, :] * g + cb_ref[...])
    u = _dot(xn_ref[...], wu_ref[...])
    act = y * (1.0 / (1.0 + jnp.exp(-y))) * u
    o_ref[...] = act.astype(o_ref.dtype)


def _ffn_up(x, gain, wg, wu, conv_w, conv_b, *, seq, tm=512, tf=512):
    n_rows, d = x.shape
    f = wg.shape[1]
    assert n_rows % tm == 0 and f % tf == 0 and seq % tm == 0 and tm % CONV_HALO == 0
    tiles_per_seq = seq // tm
    halo_blocks = tm // CONV_HALO
    kern = functools.partial(_ffn_up_kernel, tm=tm, tiles_per_seq=tiles_per_seq)
    return pl.pallas_call(
        kern,
        grid=(n_rows // tm, f // tf),
        in_specs=[
            pl.BlockSpec((tm, d), lambda i, j: (i, 0)),
            pl.BlockSpec((CONV_HALO, d), lambda i, j: (jnp.maximum(i * halo_blocks - 1, 0), 0)),
            pl.BlockSpec((1, d), lambda i, j: (0, 0)),
            pl.BlockSpec((d, tf), lambda i, j: (0, j)),
            pl.BlockSpec((d, tf), lambda i, j: (0, j)),
            pl.BlockSpec((CONV_WIDTH, tf), lambda i, j: (0, j)),
            pl.BlockSpec((1, tf), lambda i, j: (0, j)),
        ],
        out_specs=pl.BlockSpec((tm, tf), lambda i, j: (i, j)),
        out_shape=jax.ShapeDtypeStruct((n_rows, f), BF16),
        scratch_shapes=[pltpu.VMEM((tm, d), BF16), pltpu.VMEM((CONV_HALO, d), BF16),
                        pltpu.VMEM((tm + CONV_HALO, tf), F32)],
        compiler_params=_cparams("parallel", "arbitrary"),
    )(x, x, gain.reshape(1, d).astype(F32), wg, wu, conv_w.astype(F32), conv_b.reshape(1, f).astype(F32))


def _softmax_step(s, v, m_ref, l_ref, acc_ref):
    tk = s.shape[1]
    dv = v.shape[1]
    m_prev = m_ref[...]
    m_next = jnp.maximum(m_prev, jnp.max(s, axis=1, keepdims=True))
    p = jnp.exp(s - _rep(m_next, tk // LANES))
    alpha = jnp.exp(m_prev - m_next)
    l_ref[...] = alpha * l_ref[...] + jnp.sum(p, axis=1, keepdims=True)
    m_ref[...] = m_next
    acc_ref[...] = acc_ref[...] * _rep(alpha, dv // LANES) + _dot(p.astype(BF16), v)


def _causal_mask(s):
    row = lax.broadcasted_iota(jnp.int32, s.shape, 0)
    col = lax.broadcasted_iota(jnp.int32, s.shape, 1)
    return jnp.where(row >= col, s, NEG_INF)


def _init_stats(*refs):
    for m_ref, l_ref, acc_ref in zip(refs[0::3], refs[1::3], refs[2::3]):
        m_ref[...] = jnp.full(m_ref.shape, NEG_INF, F32)
        l_ref[...] = jnp.zeros(l_ref.shape, F32)
        acc_ref[...] = jnp.zeros(acc_ref.shape, F32)


def _moba_kernel(q_ref, k_ref, v_ref, km_ref, o_ref, m_ref, l_ref, acc_ref, sel_ref, *, blk, topk):
    i = pl.program_id(2)
    q = q_ref[...]
    nb = km_ref.shape[0]
    _init_stats(m_ref, l_ref, acc_ref)

    km = km_ref[...]
    km_hi = km.astype(BF16)
    km_lo = (km - km_hi.astype(F32)).astype(BF16)
    gate = _dot_t(q, km_hi) + _dot_t(q, km_lo)
    col = lax.broadcasted_iota(jnp.int32, gate.shape, 1)
    neg = jnp.float32(-jnp.inf)
    gate = jnp.where(col < i, gate, neg)
    sel = jnp.zeros(gate.shape, F32)
    for _ in range(topk):
        mx = jnp.max(gate, axis=1, keepdims=True)
        first = jnp.min(jnp.where(gate == mx, col, nb), axis=1, keepdims=True)
        pick = (col == first) & (mx > neg)
        sel = jnp.where(pick, 1.0, sel)
        gate = jnp.where(pick, neg, gate)
    sel_ref[...] = sel

    def body(j, carry):
        start = pl.multiple_of(j * blk, blk)
        kj = k_ref[pl.ds(start, blk), :]
        vj = v_ref[pl.ds(start, blk), :]
        s = _dot_t(q, kj)
        colj = lax.broadcasted_iota(jnp.int32, sel_ref.shape, 1)
        picked = jnp.max(jnp.where(colj == j, sel_ref[...], 0.0), axis=1, keepdims=True)
        s = jnp.where(picked > 0.0, s, NEG_INF)
        _softmax_step(s, vj, m_ref, l_ref, acc_ref)
        return carry

    lax.fori_loop(0, i, body, 0)

    start = pl.multiple_of(i * blk, blk)
    s = _causal_mask(_dot_t(q, k_ref[pl.ds(start, blk), :]))
    _softmax_step(s, v_ref[pl.ds(start, blk), :], m_ref, l_ref, acc_ref)
    o_ref[...] = (acc_ref[...] / l_ref[...]).astype(o_ref.dtype)


def _moba_attention(q, k, v, kmean, *, batch, seq, heads):
    blk, dh = MOBA_BLOCK, MOBA_HEAD_DIM
    assert seq % blk == 0
    nq = seq // blk
    nb = seq // blk
    kern = functools.partial(_moba_kernel, blk=blk, topk=min(MOBA_TOPK, nb))
    return pl.pallas_call(
        kern,
        grid=(batch, heads, nq),
        in_specs=[
            pl.BlockSpec((blk, dh), lambda b, h, i: (b * nq + i, h)),
            pl.BlockSpec((seq, dh), lambda b, h, i: (b, h)),
            pl.BlockSpec((seq, dh), lambda b, h, i: (b, h)),
            pl.BlockSpec((nb, dh), lambda b, h, i: (b, h)),
        ],
        out_specs=pl.BlockSpec((blk, dh), lambda b, h, i: (b * nq + i, h)),
        out_shape=jax.ShapeDtypeStruct((batch * seq, heads * dh), BF16),
        scratch_shapes=[pltpu.VMEM((blk, LANES), F32), pltpu.VMEM((blk, LANES), F32),
                        pltpu.VMEM((blk, dh), F32), pltpu.VMEM((blk, nb), F32)],
        compiler_params=_cparams("parallel", "parallel", "arbitrary"),
    )(q, k, v, kmean)


def _mla_kernel(q_ref, k_ref, v_ref, o_ref, m_ref, l_ref, acc_ref, *, tq):
    i = pl.program_id(2)
    q = q_ref[...]
    _init_stats(m_ref, l_ref, acc_ref)

    def body(j, carry):
        start = pl.multiple_of(j * tq, tq)
        s = _dot_t(q, k_ref[pl.ds(start, tq), :])
        _softmax_step(s, v_ref[pl.ds(start, tq), :], m_ref, l_ref, acc_ref)
        return carry

    lax.fori_loop(0, i, body, 0)
    start = pl.multiple_of(i * tq, tq)
    s = _causal_mask(_dot_t(q, k_ref[pl.ds(start, tq), :]))
    _softmax_step(s, v_ref[pl.ds(start, tq), :], m_ref, l_ref, acc_ref)
    o_ref[...] = (acc_ref[...] / l_ref[...]).astype(o_ref.dtype)


def _mla_attention(q, k, v, *, batch, seq, heads, tq=512):
    tq = min(tq, seq)
    assert seq % tq == 0
    nq = seq // tq
    dqk, dv = MLA_QK_PAD, MLA_V_DIM
    kern = functools.partial(_mla_kernel, tq=tq)
    return pl.pallas_call(
        kern,
        grid=(batch, heads, nq),
        in_specs=[
            pl.BlockSpec((tq, dqk), lambda b, h, i: (b * nq + i, h)),
            pl.BlockSpec((seq, dqk), lambda b, h, i: (b, h)),
            pl.BlockSpec((seq, dv), lambda b, h, i: (b, h)),
        ],
        out_specs=pl.BlockSpec((tq, dv), lambda b, h, i: (b * nq + i, h)),
        out_shape=jax.ShapeDtypeStruct((batch * seq, heads * dv), BF16),
        scratch_shapes=[pltpu.VMEM((tq, LANES), F32), pltpu.VMEM((tq, LANES), F32), pltpu.VMEM((tq, dv), F32)],
        compiler_params=_cparams("parallel", "parallel", "arbitrary"),
    )(q, k, v)


def _diff_kernel(q_ref, k_ref, v_ref, lq1_ref, lk1_ref, lq2_ref, lk2_ref, gs_ref, o_ref,
                 m0_ref, l0_ref, a0_ref, m1_ref, l1_ref, a1_ref, *, tq, dh, lambda_init):
    i = pl.program_id(2)
    q0 = q_ref[:, 0:dh]
    q1 = q_ref[:, dh:2 * dh]
    _init_stats(m0_ref, l0_ref, a0_ref, m1_ref, l1_ref, a1_ref)

    def step(start, mask):
        kc = k_ref[pl.ds(start, tq), :]
        vc = v_ref[pl.ds(start, tq), :]
        s0 = _dot_t(q0, kc[:, 0:dh])
        s1 = _dot_t(q1, kc[:, dh:2 * dh])
        if mask:
            s0, s1 = _causal_mask(s0), _causal_mask(s1)
        _softmax_step(s0, vc, m0_ref, l0_ref, a0_ref)
        _softmax_step(s1, vc, m1_ref, l1_ref, a1_ref)

    def body(j, carry):
        step(pl.multiple_of(j * tq, tq), False)
        return carry

    lax.fori_loop(0, i, body, 0)
    step(pl.multiple_of(i * tq, tq), True)

    lam = (jnp.exp(jnp.sum(lq1_ref[...] * lk1_ref[...], axis=1, keepdims=True))
           - jnp.exp(jnp.sum(lq2_ref[...] * lk2_ref[...], axis=1, keepdims=True)) + lambda_init)
    reps = a0_ref.shape[1] // LANES
    o = a0_ref[...] / _rep(l0_ref[...], reps) - lam * (a1_ref[...] / _rep(l1_ref[...], reps))
    ms = jnp.mean(o * o, axis=1, keepdims=True)
    o = o * lax.rsqrt(ms + NORM_EPS) * gs_ref[...] * (1.0 - lambda_init)
    o_ref[...] = o.astype(o_ref.dtype)


def _diff_attention(q, k, v, lq1, lk1, lq2, lk2, g_sub, lambda_init, *, batch, seq, heads, tq=512):
    tq = min(tq, seq)
    assert seq % tq == 0
    nq = seq // tq
    dh = DIFF_HEAD_DIM
    dv = 2 * dh
    kern = functools.partial(_diff_kernel, tq=tq, dh=dh, lambda_init=lambda_init)
    vec = lambda a: a.reshape(1, -1).astype(F32)
    small = lambda n: pl.BlockSpec((1, n), lambda b, h, i: (0, 0))
    return pl.pallas_call(
        kern,
        grid=(batch, heads, nq),
        in_specs=[
            pl.BlockSpec((tq, dv), lambda b, h, i: (b * nq + i, h)),
            pl.BlockSpec((seq, dv), lambda b, h, i: (b, h)),
            pl.BlockSpec((seq, dv), lambda b, h, i: (b, h)),
            small(dh), small(dh), small(dh), small(dh), small(dv),
        ],
        out_specs=pl.BlockSpec((tq, dv), lambda b, h, i: (b * nq + i, h)),
        out_shape=jax.ShapeDtypeStruct((batch * seq, heads * dv), BF16),
        scratch_shapes=[pltpu.VMEM((tq, LANES), F32), pltpu.VMEM((tq, LANES), F32), pltpu.VMEM((tq, dv), F32),
                        pltpu.VMEM((tq, LANES), F32), pltpu.VMEM((tq, LANES), F32), pltpu.VMEM((tq, dv), F32)],
        compiler_params=_cparams("parallel", "parallel", "arbitrary"),
    )(q, k, v, vec(lq1), vec(lk1), vec(lq2), vec(lk2), vec(g_sub))


def _swa_kernel(sink_ref, q_ref, k_ref, v_ref, o_ref, *, tq, window, group):
    kh = pl.program_id(1)
    i = pl.program_id(2)
    half = LANES // 2
    n_sub = tq // window
    pairs = group // 2

    def sub(sb, carry):
        r0 = i * tq + sb * window
        k0 = jnp.maximum(r0 - window, 0)
        qrow = pl.multiple_of(sb * window, window)
        kstart = pl.multiple_of(k0, window)
        kk = k_ref[pl.ds(kstart, 2 * window), :]
        vv = v_ref[pl.ds(kstart, 2 * window), :]
        qpos = r0 + lax.broadcasted_iota(jnp.int32, (window, 2 * window), 0)
        kpos = k0 + lax.broadcasted_iota(jnp.int32, (window, 2 * window), 1)
        valid = (kpos <= qpos) & (kpos > qpos - window)
        lane_lo = lax.broadcasted_iota(jnp.int32, (window, LANES), 1) < half
        for pr in range(pairs):
            q2 = q_ref[pl.ds(qrow, window), pr * LANES:(pr + 1) * LANES]
            outs = []
            for e in range(2):
                keep = lane_lo if e == 0 else jnp.logical_not(lane_lo)
                qe = jnp.where(keep, q2, jnp.zeros_like(q2))
                s = jnp.where(valid, _dot_t(qe, kk), NEG_INF)
                sink = sink_ref[kh * group + 2 * pr + e]
                m = jnp.maximum(jnp.max(s, axis=1, keepdims=True), sink)
                p = jnp.exp(s - m)
                denom = jnp.sum(p, axis=1, keepdims=True) + jnp.exp(sink - m)
                p = p / denom
                outs.append(_dot(p.astype(BF16), vv))
            o_pair = jnp.where(lane_lo, outs[0], outs[1])
            o_ref[pl.ds(qrow, window), pr * LANES:(pr + 1) * LANES] = o_pair.astype(o_ref.dtype)
        return carry

    lax.fori_loop(0, n_sub, sub, 0)


def _swa_attention(q, k, v, sinks, *, batch, seq, kv_heads, tq=512):
    tq = min(tq, seq)
    assert seq % tq == 0 and tq % SWA_WINDOW == 0 and seq >= 2 * SWA_WINDOW
    nq = seq // tq
    gw = SWA_GROUP * SWA_HEAD_DIM
    kern = functools.partial(_swa_kernel, tq=tq, window=SWA_WINDOW, group=SWA_GROUP)
    return pl.pallas_call(
        kern,
        grid=(batch, kv_heads, nq),
        in_specs=[
            pl.BlockSpec(memory_space=pltpu.SMEM),
            pl.BlockSpec((tq, gw), lambda b, h, i: (b * nq + i, h)),
            pl.BlockSpec((seq, LANES), lambda b, h, i: (b, h)),
            pl.BlockSpec((seq, LANES), lambda b, h, i: (b, h)),
        ],
        out_specs=pl.BlockSpec((tq, gw), lambda b, h, i: (b * nq + i, h)),
        out_shape=jax.ShapeDtypeStruct((batch * seq, kv_heads * gw), BF16),
        compiler_params=_cparams("parallel", "parallel", "arbitrary"),
    )(sinks.astype(F32), q, k, v)


def _tile_gain(g, n):
    return jnp.tile(g.astype(F32), n)


def _moba_layer(x, an, wq, wk, wv, gq, gk, wo, *, batch, seq):
    d = x.shape[1]
    dh = MOBA_HEAD_DIM
    heads = wq.shape[1] // dh
    rot = dh // ROT_FRACTION
    rope = _rope_tables(seq, LANES, rot)
    norm_rope = [("norm", dh, False, True)]
    q = _norm_proj(x, an, wq.astype(BF16), pattern=norm_rope, seq=seq, hg=_tile_gain(gq, heads), rope=rope,
                   rope_half=rot // 2, out_scale=dh ** -0.5)
    k, kmean = _norm_proj(x, an, wk.astype(BF16), pattern=norm_rope, seq=seq, hg=_tile_gain(gk, heads), rope=rope,
                          rope_half=rot // 2, kmean_rows=MOBA_BLOCK)
    v = _norm_proj(x, an, wv.astype(BF16), pattern=[("copy",)], seq=seq)
    kmean = kmean.reshape(kmean.shape[0], kmean.shape[2])
    o = _moba_attention(q, k, v, kmean, batch=batch, seq=seq, heads=heads)
    return _matmul_residual(o, wo.astype(BF16), x)


def _mla_layer(x, an, wq_a, g_qa, wq_b, wkv_a, g_kva, wkv_b, g_qn, g_kn, g_qr, g_kr, wo, *, batch, seq):
    d = x.shape[1]
    nope, rd, vd, pad = MLA_NOPE_DIM, MLA_ROPE_DIM, MLA_V_DIM, MLA_QK_PAD
    q_rank = wq_a.shape[1]
    kv_rank = wkv_a.shape[1] - rd
    heads = wq_b.shape[1] // (nope + rd)
    rope = _rope_tables(seq, LANES, rd)
    zpad = lambda n: jnp.zeros((n,), F32)

    w_a = jnp.concatenate([wq_a, wkv_a, jnp.zeros((d, LANES - rd), F32)], axis=1).astype(BF16)
    n_a = w_a.shape[1]
    chunks_a = n_a // LANES
    hg_a = jnp.concatenate([jnp.ones((q_rank + kv_rank,), F32), g_kr.astype(F32), zpad(LANES - rd)])
    pat_a = [("copy",)] * (chunks_a - 1) + [("norm", rd, False, True)]
    lat = _norm_proj(x, an, w_a, pattern=pat_a, seq=seq, hg=hg_a, rope=rope, rope_half=rd // 2,
                     out_dtype=F32, tn=n_a)

    wq_b3 = wq_b.reshape(q_rank, heads, nope + rd)
    wq_p = jnp.concatenate([wq_b3, jnp.zeros((q_rank, heads, pad - nope - rd), F32)], axis=2)
    wq_p = wq_p.reshape(q_rank, heads * pad).astype(BF16)
    hg_q = jnp.tile(jnp.concatenate([g_qn.astype(F32), g_qr.astype(F32), zpad(pad - nope - rd)]), heads)
    pat_q = [("norm", nope, False, False), ("norm", rd, False, True)]
    q = _norm_proj(lat, g_qa, wq_p, pattern=pat_q, seq=seq, xcol=0, hg=hg_q, rope=rope, rope_half=rd // 2,
                   out_scale=(nope + rd) ** -0.5)

    wkv3 = wkv_b.reshape(kv_rank, heads, nope + vd)
    w_k = wkv3[:, :, :nope].reshape(kv_rank, heads * nope).astype(BF16)
    w_v = wkv3[:, :, nope:].reshape(kv_rank, heads * vd).astype(BF16)
    assert q_rank == kv_rank
    kr_col = (q_rank + kv_rank) // LANES
    k = _norm_proj(lat, g_kva, w_k, pattern=[("norm", nope, False, False)], seq=seq, xcol=1,
                   hg=_tile_gain(g_kn, heads), mla_kr=(lat, kr_col))
    v = _norm_proj(lat, g_kva, w_v, pattern=[("copy",)], seq=seq, xcol=1)
    o = _mla_attention(q, k, v, batch=batch, seq=seq, heads=heads)
    return _matmul_residual(o, wo.astype(BF16), x)


def _swa_layer(x, an, wq, wk, wv, gq, gk, sinks, wo, *, batch, seq):
    d = x.shape[1]
    dh = SWA_HEAD_DIM
    hq = wq.shape[1] // dh
    hkv = wk.shape[1] // dh
    rot = dh // ROT_FRACTION
    rope = _rope_tables(seq, dh, rot)

    def dup(w):
        w3 = w.reshape(d, hkv, dh)
        return jnp.concatenate([w3, w3], axis=2).reshape(d, hkv * 2 * dh).astype(BF16)

    q = _norm_proj(x, an, wq.astype(BF16), pattern=[("norm", dh, True, True)], seq=seq, hg=_tile_gain(gq, hq),
                   rope=rope, rope_half=rot // 2, out_scale=dh ** -0.5)
    k = _norm_proj(x, an, dup(wk), pattern=[("norm", 2 * dh, False, True)], seq=seq,
                   hg=_tile_gain(gk, 2 * hkv), rope=rope, rope_half=rot // 2)
    v = _norm_proj(x, an, dup(wv), pattern=[("copy",)], seq=seq)
    o = _swa_attention(q, k, v, sinks, batch=batch, seq=seq, kv_heads=hkv)
    return _matmul_residual(o, wo.astype(BF16), x)


def _diff_layer(x, an, wq, wk, wv, gq, gk, lq1, lk1, lq2, lk2, g_sub, wo, lambda_init, *, batch, seq):
    dh = DIFF_HEAD_DIM
    heads = wq.shape[1] // (2 * dh)
    rot = dh // ROT_FRACTION
    rope = _rope_tables(seq, LANES, rot)
    norm_rope = [("norm", dh, False, True)]
    q = _norm_proj(x, an, wq.astype(BF16), pattern=norm_rope, seq=seq, hg=_tile_gain(gq, 2 * heads), rope=rope,
                   rope_half=rot // 2, out_scale=dh ** -0.5)
    k = _norm_proj(x, an, wk.astype(BF16), pattern=norm_rope, seq=seq, hg=_tile_gain(gk, 2 * heads), rope=rope,
                   rope_half=rot // 2)
    v = _norm_proj(x, an, wv.astype(BF16), pattern=[("copy",)], seq=seq)
    o = _diff_attention(q, k, v, lq1, lk1, lq2, lk2, g_sub, lambda_init, batch=batch, seq=seq, heads=heads)
    return _matmul_residual(o, wo.astype(BF16), x)


def _ffn_layer(x, fn, w_gate, w_up, conv_w, conv_b, w_down, *, seq):
    act = _ffn_up(x, fn, w_gate.astype(BF16), w_up.astype(BF16), conv_w, conv_b, seq=seq)
    return _matmul_residual(act, w_down.astype(BF16), x)


def kernel(x, attn_norm, ffn_norm, moba_wq, moba_wk, moba_wv, moba_gq, moba_gk, moba_wo, mla_wq_a, mla_g_qa, mla_wq_b, mla_wkv_a, mla_g_kva, mla_wkv_b, mla_g_qn, mla_g_kn, mla_g_qr, mla_g_kr, mla_wo, swa_wq, swa_wk, swa_wv, swa_gq, swa_gk, swa_sinks, swa_wo, diff_wq, diff_wk, diff_wv, diff_gq, diff_gk, diff_lq1, diff_lk1, diff_lq2, diff_lk2, diff_g_sub, diff_wo, ffn_w_gate, ffn_w_up, ffn_conv_w, ffn_conv_b, ffn_w_down):
    batch, seq, d = x.shape
    h = x.reshape(batch * seq, d)
    kw = dict(batch=batch, seq=seq)
    for i in range(DEPTH):
        m, j = i % 4, i // 4
        if m == 0:
            h = _moba_layer(h, attn_norm[i], moba_wq[j], moba_wk[j], moba_wv[j], moba_gq[j], moba_gk[j],
                            moba_wo[j], **kw)
        elif m == 1:
            h = _mla_layer(h, attn_norm[i], mla_wq_a[j], mla_g_qa[j], mla_wq_b[j], mla_wkv_a[j], mla_g_kva[j],
                           mla_wkv_b[j], mla_g_qn[j], mla_g_kn[j], mla_g_qr[j], mla_g_kr[j], mla_wo[j], **kw)
        elif m == 2:
            h = _swa_layer(h, attn_norm[i], swa_wq[j], swa_wk[j], swa_wv[j], swa_gq[j], swa_gk[j],
                           swa_sinks[j], swa_wo[j], **kw)
        else:
            lambda_init = 0.8 - 0.6 * math.exp(-0.3 * i)
            h = _diff_layer(h, attn_norm[i], diff_wq[j], diff_wk[j], diff_wv[j], diff_gq[j], diff_gk[j],
                            diff_lq1[j], diff_lk1[j], diff_lq2[j], diff_lk2[j], diff_g_sub[j], diff_wo[j],
                            lambda_init, **kw)
        h = _ffn_layer(h, ffn_norm[i], ffn_w_gate[i], ffn_w_up[i], ffn_conv_w[i], ffn_conv_b[i],
                       ffn_w_down[i], seq=seq)
    return h.reshape(batch, seq, d)
```

```python
import functools
import math

import jax
import jax.numpy as jnp
from jax import lax
from jax.experimental import pallas as pl
from jax.experimental.pallas import tpu as pltpu

F32 = jnp.float32
BF16 = jnp.bfloat16

LANES = 128
NORM_EPS = 1e-6
NEG_INF = -1e30
ROPE_THETA = 500000.0
ROT_FRACTION = 4
DEPTH = 4

MOBA_HEAD_DIM = 128
MOBA_BLOCK = 256
MOBA_TOPK = 3
MOBA_KEY_CHUNK = 1024

MLA_NOPE_DIM = 128
MLA_ROPE_DIM = 64
MLA_V_DIM = 128
MLA_QK_PAD = 256

SWA_HEAD_DIM = 64
SWA_GROUP = 8
SWA_WINDOW = 128

DIFF_HEAD_DIM = 128

CONV_WIDTH = 3
CONV_HALO = 8

VT_CHUNK = 256
VMEM_LIMIT = 56 * 1024 * 1024


def _cparams(*sem):
    return pltpu.CompilerParams(dimension_semantics=sem, vmem_limit_bytes=VMEM_LIMIT)


def _dot(a, b):
    return jnp.dot(a, b, preferred_element_type=F32)


def _dot_t(a, b):
    return lax.dot_general(a, b, (((1,), (1,)), ((), ())), preferred_element_type=F32)


def _rep(x, n):
    return x if n == 1 else jnp.concatenate([x] * n, axis=1)


def _rope_tables(seq, seg, rot):
    half = rot // 2
    inv = ROPE_THETA ** (-jnp.arange(half, dtype=F32) / half)
    ang = jnp.arange(seq, dtype=jnp.int32).astype(F32)[:, None] * inv[None, :]
    cos, sin = jnp.cos(ang), jnp.sin(ang)
    ones = jnp.ones((seq, seg - rot), F32)
    zeros_h = jnp.zeros((seq, half), F32)
    zeros_r = jnp.zeros((seq, seg - rot), F32)
    c = jnp.concatenate([cos, cos, ones], axis=1)
    s1 = jnp.concatenate([-sin, zeros_h, zeros_r], axis=1)
    s2 = jnp.concatenate([zeros_h, sin, zeros_r], axis=1)
    reps = LANES // seg
    return tuple(jnp.tile(t, (1, reps)) for t in (c, s1, s2))


def _rms_rows_to_bf16(x_ref, g_ref, xn_ref):
    x = x_ref[...]
    ms = jnp.mean(x * x, axis=-1, keepdims=True)
    xn_ref[...] = (x * lax.rsqrt(ms + NORM_EPS) * g_ref[...]).astype(BF16)


def _proj_kernel(*refs, pattern, rope_half, out_scale, n_chunks, has_rope, has_hg, kmean_rows, has_pair):
    it = iter(refs)
    x_ref, g_ref, w_ref = next(it), next(it), next(it)
    hg_ref = next(it) if has_hg else None
    if has_rope:
        c_ref, s1_ref, s2_ref = next(it), next(it), next(it)
    pair_ref = next(it) if has_pair else None
    o_ref = next(it)
    km_ref = next(it) if kmean_rows else None
    xn_ref = next(it)

    @pl.when(pl.program_id(1) == 0)
    def _():
        _rms_rows_to_bf16(x_ref, g_ref, xn_ref)

    acc = _dot(xn_ref[...], w_ref[...])
    for c in range(n_chunks):
        mode = pattern[c % len(pattern)]
        blk = acc[:, c * LANES:(c + 1) * LANES]
        if mode[0] == "norm":
            _, width, two, rope = mode
            sq = blk * blk
            if two:
                lo = lax.broadcasted_iota(jnp.int32, blk.shape, 1) < (LANES // 2)
                s_lo = jnp.sum(jnp.where(lo, sq, 0.0), axis=1, keepdims=True)
                s_hi = jnp.sum(jnp.where(lo, 0.0, sq), axis=1, keepdims=True)
                r = jnp.where(lo, lax.rsqrt(s_lo * (1.0 / width) + NORM_EPS),
                              lax.rsqrt(s_hi * (1.0 / width) + NORM_EPS))
            else:
                r = lax.rsqrt(jnp.sum(sq, axis=1, keepdims=True) * (1.0 / width) + NORM_EPS)
            blk = blk * r * hg_ref[:, c * LANES:(c + 1) * LANES]
            if rope:
                blk = (blk * c_ref[...]
                       + pltpu.roll(blk, LANES - rope_half, 1) * s1_ref[...]
                       + pltpu.roll(blk, rope_half, 1) * s2_ref[...])
            if out_scale != 1.0:
                blk = blk * out_scale
        if has_pair:
            o_ref[:, (2 * c) * LANES:(2 * c + 1) * LANES] = blk.astype(o_ref.dtype)
            o_ref[:, (2 * c + 1) * LANES:(2 * c + 2) * LANES] = pair_ref[...].astype(o_ref.dtype)
        else:
            o_ref[:, c * LANES:(c + 1) * LANES] = blk.astype(o_ref.dtype)
        if kmean_rows:
            for r_i in range(blk.shape[0] // kmean_rows):
                part = blk[r_i * kmean_rows:(r_i + 1) * kmean_rows, :]
                km_ref[r_i:r_i + 1, :, c * LANES:(c + 1) * LANES] = (
                    jnp.sum(part, axis=0, keepdims=True) * (1.0 / kmean_rows))[None]


def _norm_proj(x, gain, w, *, pattern, seq, name, xcol=0, hg=None, rope=None, rope_half=0, out_scale=1.0,
               out_dtype=None, kmean_rows=0, pair=None, tm=512, tn=512):
    n_rows = x.shape[0]
    k_dim, n_out = w.shape
    tn = min(tn, n_out)
    out_dtype = out_dtype or BF16
    assert n_rows % tm == 0 and n_out % tn == 0 and seq % tm == 0
    n_chunks = tn // LANES
    assert n_chunks % len(pattern) == 0
    tiles_per_seq = seq // tm
    in_specs = [
        pl.BlockSpec((tm, k_dim), lambda i, j: (i, xcol)),
        pl.BlockSpec((1, k_dim), lambda i, j: (0, 0)),
        pl.BlockSpec((k_dim, tn), lambda i, j: (0, j)),
    ]
    args = [x, gain.reshape(1, k_dim).astype(F32), w]
    if hg is not None:
        in_specs.append(pl.BlockSpec((1, tn), lambda i, j: (0, j)))
        args.append(hg.reshape(1, n_out).astype(F32))
    if rope is not None:
        for t in rope:
            in_specs.append(pl.BlockSpec((tm, LANES), lambda i, j: (i % tiles_per_seq, 0)))
            args.append(t)
    out_mult = 1
    if pair is not None:
        pair_arr, pair_map = pair
        in_specs.append(pl.BlockSpec((tm, LANES), pair_map))
        args.append(pair_arr)
        out_mult = 2
    out_shape = [jax.ShapeDtypeStruct((n_rows, n_out * out_mult), out_dtype)]
    out_specs = [pl.BlockSpec((tm, tn * out_mult), lambda i, j: (i, j))]
    if kmean_rows:
        out_shape.append(jax.ShapeDtypeStruct((n_rows // kmean_rows, 1, n_out), F32))
        out_specs.append(pl.BlockSpec((tm // kmean_rows, 1, tn), lambda i, j: (i, 0, j)))
    kern = functools.partial(
        _proj_kernel, pattern=tuple(pattern), rope_half=rope_half, out_scale=out_scale, n_chunks=n_chunks,
        has_rope=rope is not None, has_hg=hg is not None, kmean_rows=kmean_rows, has_pair=pair is not None)
    res = pl.pallas_call(
        kern,
        grid=(n_rows // tm, n_out // tn),
        in_specs=in_specs,
        out_specs=out_specs,
        out_shape=out_shape,
        scratch_shapes=[pltpu.VMEM((tm, k_dim), BF16)],
        compiler_params=_cparams("parallel", "arbitrary"),
        name=name,
    )(*args)
    return res if kmean_rows else res[0]


def _proj_t_kernel(*refs, pattern, rope_half, out_scale, n_chunks, has_rope, has_hg, pos_chunk):
    it = iter(refs)
    x_ref, g_ref, w_ref = next(it), next(it), next(it)
    hg_ref = next(it) if has_hg else None
    if has_rope:
        c_ref, s1_ref, s2_ref = next(it), next(it), next(it)
    o_ref = next(it)
    xn_ref = next(it)

    @pl.when(pl.program_id(1) == 0)
    def _():
        _rms_rows_to_bf16(x_ref, g_ref, xn_ref)

    acc = _dot_t(w_ref[...], xn_ref[...])
    tm = acc.shape[1]
    for c in range(n_chunks):
        mode = pattern[c % len(pattern)]
        blk = acc[c * LANES:(c + 1) * LANES, :]
        if mode[0] == "norm":
            _, width, rope = mode
            r = lax.rsqrt(jnp.sum(blk * blk, axis=0, keepdims=True) * (1.0 / width) + NORM_EPS)
            blk = blk * r * _rep(hg_ref[c * LANES:(c + 1) * LANES, :], tm // LANES)
            if rope:
                up = jnp.concatenate([blk[rope_half:, :], blk[:rope_half, :]], axis=0)
                dn = jnp.concatenate([blk[-rope_half:, :], blk[:-rope_half, :]], axis=0)
                blk = blk * c_ref[...] + up * s1_ref[...] + dn * s2_ref[...]
            if out_scale != 1.0:
                blk = blk * out_scale
        for pc in range(tm // pos_chunk):
            o_ref[pc, c * LANES:(c + 1) * LANES, :] = blk[:, pc * pos_chunk:(pc + 1) * pos_chunk].astype(o_ref.dtype)


def _norm_proj_t(x, gain, w_t, *, pattern, seq, name, pos_chunk, xcol=0, hg=None, rope=None, rope_half=0,
                 out_scale=1.0, tm=512, tn=512):
    n_rows = x.shape[0]
    n_out, k_dim = w_t.shape
    tn = min(tn, n_out)
    assert n_rows % tm == 0 and n_out % tn == 0 and seq % tm == 0 and tm % pos_chunk == 0
    n_chunks = tn // LANES
    assert n_chunks % len(pattern) == 0
    tiles_per_seq = seq // tm
    in_specs = [
        pl.BlockSpec((tm, k_dim), lambda i, j: (i, xcol)),
        pl.BlockSpec((1, k_dim), lambda i, j: (0, 0)),
        pl.BlockSpec((tn, k_dim), lambda i, j: (j, 0)),
    ]
    args = [x, gain.reshape(1, k_dim).astype(F32), w_t]
    if hg is not None:
        in_specs.append(pl.BlockSpec((tn, LANES), lambda i, j: (j, 0)))
        args.append(jnp.broadcast_to(hg.astype(F32)[:, None], (n_out, LANES)))
    if rope is not None:
        for t in rope:
            in_specs.append(pl.BlockSpec((LANES, tm), lambda i, j: (0, i % tiles_per_seq)))
            args.append(t.T)
    kern = functools.partial(
        _proj_t_kernel, pattern=tuple(pattern), rope_half=rope_half, out_scale=out_scale, n_chunks=n_chunks,
        has_rope=rope is not None, has_hg=hg is not None, pos_chunk=pos_chunk)
    per_tile = tm // pos_chunk
    return pl.pallas_call(
        kern,
        grid=(n_rows // tm, n_out // tn),
        in_specs=in_specs,
        out_specs=pl.BlockSpec((per_tile, tn, pos_chunk), lambda i, j: (i, j, 0)),
        out_shape=jax.ShapeDtypeStruct((n_rows // pos_chunk, n_out, pos_chunk), BF16),
        scratch_shapes=[pltpu.VMEM((tm, k_dim), BF16)],
        compiler_params=_cparams("parallel", "arbitrary"),
        name=name,
    )(*args)


def _mm_res_kernel(a_ref, w_ref, r_ref, o_ref):
    o_ref[...] = r_ref[...] + _dot(a_ref[...], w_ref[...])


def _matmul_residual(a, w, res, *, name, tm=512, tn=512):
    n_rows, k_dim = a.shape
    n_out = w.shape[1]
    assert n_rows % tm == 0 and n_out % tn == 0
    return pl.pallas_call(
        _mm_res_kernel,
        grid=(n_rows // tm, n_out // tn),
        in_specs=[
            pl.BlockSpec((tm, k_dim), lambda i, j: (i, 0)),
            pl.BlockSpec((k_dim, tn), lambda i, j: (0, j)),
            pl.BlockSpec((tm, tn), lambda i, j: (i, j)),
        ],
        out_specs=pl.BlockSpec((tm, tn), lambda i, j: (i, j)),
        out_shape=jax.ShapeDtypeStruct((n_rows, n_out), F32),
        compiler_params=_cparams("parallel", "parallel"),
        name=name,
    )(a, w, res)


def _ffn_up_kernel(x_ref, xh_ref, g_ref, wg_ref, wu_ref, cw_ref, cb_ref, o_ref, xn_ref, xhn_ref, gs_ref,
                   *, tm, tiles_per_seq):
    i = pl.program_id(0)

    @pl.when(pl.program_id(1) == 0)
    def _():
        _rms_rows_to_bf16(x_ref, g_ref, xn_ref)
        _rms_rows_to_bf16(xh_ref, g_ref, xhn_ref)

    wg = wg_ref[...]
    g = _dot(xn_ref[...], wg)
    gh = _dot(xhn_ref[...], wg)
    seq_start = (i % tiles_per_seq) == 0
    gs_ref[0:CONV_HALO, :] = jnp.where(seq_start, 0.0, gh)
    gs_ref[CONV_HALO:CONV_HALO + tm, :] = g
    y = (cw_ref[0:1, :] * gs_ref[CONV_HALO - 2:CONV_HALO - 2 + tm, :]
         + cw_ref[1:2, :] * gs_ref[CONV_HALO - 1:CONV_HALO - 1 + tm, :]
         + cw_ref[2:3, :] * g + cb_ref[...])
    u = _dot(xn_ref[...], wu_ref[...])
    act = y * (1.0 / (1.0 + jnp.exp(-y))) * u
    o_ref[...] = act.astype(o_ref.dtype)


def _ffn_up(x, gain, wg, wu, conv_w, conv_b, *, seq, tm=512, tf=512):
    n_rows, d = x.shape
    f = wg.shape[1]
    assert n_rows % tm == 0 and f % tf == 0 and seq % tm == 0 and tm % CONV_HALO == 0
    tiles_per_seq = seq // tm
    halo_blocks = tm // CONV_HALO
    kern = functools.partial(_ffn_up_kernel, tm=tm, tiles_per_seq=tiles_per_seq)
    return pl.pallas_call(
        kern,
        grid=(n_rows // tm, f // tf),
        in_specs=[
            pl.BlockSpec((tm, d), lambda i, j: (i, 0)),
            pl.BlockSpec((CONV_HALO, d), lambda i, j: (jnp.maximum(i * halo_blocks - 1, 0), 0)),
            pl.BlockSpec((1, d), lambda i, j: (0, 0)),
            pl.BlockSpec((d, tf), lambda i, j: (0, j)),
            pl.BlockSpec((d, tf), lambda i, j: (0, j)),
            pl.BlockSpec((CONV_WIDTH, tf), lambda i, j: (0, j)),
            pl.BlockSpec((1, tf), lambda i, j: (0, j)),
        ],
        out_specs=pl.BlockSpec((tm, tf), lambda i, j: (i, j)),
        out_shape=jax.ShapeDtypeStruct((n_rows, f), BF16),
        scratch_shapes=[pltpu.VMEM((tm, d), BF16), pltpu.VMEM((CONV_HALO, d), BF16),
                        pltpu.VMEM((tm + CONV_HALO, tf), F32)],
        compiler_params=_cparams("parallel", "arbitrary"),
        name="ffn_up",
    )(x, x, gain.reshape(1, d).astype(F32), wg, wu, conv_w.astype(F32), conv_b.reshape(1, f).astype(F32))


def _flash_t_kernel(*refs, tq, ck, n_comp, comp_dk, moba_topk, diff_lambda_init):
    it = iter(refs)
    q_ref, k_ref, v_ref = next(it), next(it), next(it)
    km_ref = next(it) if moba_topk else None
    if diff_lambda_init is not None:
        lq1_ref, lk1_ref, lq2_ref, lk2_ref, gs_ref = (next(it) for _ in range(5))
    o_ref = next(it)
    s_ref, p_ref, a_ref, m_ref, l_ref, acc_ref = (next(it) for _ in range(6))

    i = pl.program_id(2)
    qt = q_ref[0]
    m_ref[...] = jnp.full(m_ref.shape, NEG_INF, F32)
    l_ref[...] = jnp.zeros(l_ref.shape, F32)
    acc_ref[...] = jnp.zeros(acc_ref.shape, F32)

    if moba_topk:
        nb = km_ref.shape[0]
        km = km_ref[...]
        km_hi = km.astype(BF16)
        km_lo = (km - km_hi.astype(F32)).astype(BF16)
        gate = _dot(km_hi, qt) + _dot(km_lo, qt)
        row = lax.broadcasted_iota(jnp.int32, gate.shape, 0)
        neg = jnp.float32(-jnp.inf)
        gate = jnp.where(row < i, gate, neg)
        allowed = row == i
        for _ in range(moba_topk):
            mx = jnp.max(gate, axis=0, keepdims=True)
            first = jnp.min(jnp.where(gate == mx, row, nb), axis=0, keepdims=True)
            pick = (row == first) & (mx > neg)
            allowed = allowed | pick
            gate = jnp.where(pick, neg, gate)
        bias = jnp.where(allowed, 0.0, NEG_INF)
        bias = jnp.concatenate([bias, jnp.zeros((LANES - nb, tq), F32)], axis=0).astype(BF16)
        q_parts = [jnp.concatenate([qt, bias], axis=0)]
    else:
        q_parts = [qt[c * comp_dk:(c + 1) * comp_dk, :] for c in range(n_comp)]

    slabs = ck // VT_CHUNK

    def qk(chunk, slot):
        start = pl.multiple_of(chunk * ck, ck)
        kc = k_ref[pl.ds(start, ck), :]
        for c in range(n_comp):
            s_ref[slot * n_comp + c] = _dot(kc[:, c * comp_dk:(c + 1) * comp_dk], q_parts[c])

    def sm(chunk, slot, masked):
        for c in range(n_comp):
            s = s_ref[slot * n_comp + c]
            if masked:
                key = chunk * ck + lax.broadcasted_iota(jnp.int32, s.shape, 0)
                qry = i * tq + lax.broadcasted_iota(jnp.int32, s.shape, 1)
                s = jnp.where(key <= qry, s, NEG_INF)
            m_prev = m_ref[c]
            m_next = jnp.maximum(m_prev, jnp.max(s, axis=0, keepdims=True))
            p = jnp.exp(s - m_next)
            alpha = jnp.exp(m_prev - m_next)
            l_ref[c] = alpha * l_ref[c] + jnp.sum(p, axis=0, keepdims=True)
            m_ref[c] = m_next
            a_ref[slot * n_comp + c] = alpha
            p_ref[slot * n_comp + c] = p.astype(BF16)

    def pv(chunk, slot):
        vt = [v_ref[chunk * slabs + t] for t in range(slabs)]
        vt = vt[0] if slabs == 1 else jnp.concatenate(vt, axis=1)
        for c in range(n_comp):
            acc_ref[c] = acc_ref[c] * a_ref[slot * n_comp + c] + _dot(vt, p_ref[slot * n_comp + c])

    n = (i * tq) // ck + 1
    qk(0, 0)

    @pl.when(n >= 2)
    def _():
        sm(0, 0, False)
        qk(1, 1)

    def pair_body(u, carry):
        c0 = 2 * u
        qk(c0 + 2, 0)
        pv(c0, 0)
        sm(c0 + 1, 1, False)
        qk(c0 + 3, 1)
        pv(c0 + 1, 1)
        sm(c0 + 2, 0, False)
        return carry

    pairs = jnp.maximum(n - 2, 0) // 2
    lax.fori_loop(0, pairs, pair_body, 0)

    @pl.when((n % 2 == 1) & (n >= 3))
    def _():
        qk(n - 1, 0)
        pv(n - 3, 0)
        sm(n - 2, 1, False)
        pv(n - 2, 1)

    @pl.when(n % 2 == 1)
    def _():
        sm(n - 1, 0, True)
        pv(n - 1, 0)

    @pl.when(n % 2 == 0)
    def _():
        pv(n - 2, 0)
        sm(n - 1, 1, True)
        pv(n - 1, 1)

    outs = [acc_ref[c] / l_ref[c] for c in range(n_comp)]
    if diff_lambda_init is None:
        o = outs[0]
    else:
        lam = (jnp.exp(jnp.sum(lq1_ref[...] * lk1_ref[...], axis=1, keepdims=True))
               - jnp.exp(jnp.sum(lq2_ref[...] * lk2_ref[...], axis=1, keepdims=True)) + diff_lambda_init)
        o = outs[0] - lam * outs[1]
        ms = jnp.mean(o * o, axis=0, keepdims=True)
        o = o * lax.rsqrt(ms + NORM_EPS) * _rep(gs_ref[...], tq // LANES) * (1.0 - diff_lambda_init)
    o_ref[...] = o.T.astype(o_ref.dtype)


def _flash_t(q_t, k, v_t, *, batch, seq, heads, tq, ck, dk, dv, name, n_comp=1, kmean=None, diff=None):
    assert seq % ck == 0 and ck % tq == 0 and tq % VT_CHUNK == 0
    nq = seq // tq
    dq = q_t.shape[1] // heads
    assert q_t.shape == (batch * nq, heads * dq, tq) and k.shape == (batch * seq, heads * dk)
    assert v_t.shape == (batch * seq // VT_CHUNK, heads * dv, VT_CHUNK)
    slabs_per_seq = seq // VT_CHUNK
    in_specs = [
        pl.BlockSpec((1, dq, tq), lambda b, h, i: (b * nq + i, h, 0)),
        pl.BlockSpec((seq, dk), lambda b, h, i: (b, h)),
        pl.BlockSpec((slabs_per_seq, dv, VT_CHUNK), lambda b, h, i: (b, h, 0)),
    ]
    args = [q_t, k, v_t]
    moba_topk = 0
    if kmean is not None:
        nb = seq // MOBA_BLOCK
        assert tq == MOBA_BLOCK and nb <= LANES and dq + LANES == dk
        moba_topk = min(MOBA_TOPK, nb)
        in_specs.append(pl.BlockSpec((nb, dq), lambda b, h, i: (b, h)))
        args.append(kmean)
    lambda_init = None
    if diff is not None:
        lq1, lk1, lq2, lk2, g_sub, lambda_init = diff
        for a in (lq1, lk1, lq2, lk2):
            in_specs.append(pl.BlockSpec((1, a.shape[0]), lambda b, h, i: (0, 0)))
            args.append(a.reshape(1, -1).astype(F32))
        in_specs.append(pl.BlockSpec((dv, LANES), lambda b, h, i: (0, 0)))
        args.append(jnp.broadcast_to(g_sub.astype(F32)[:, None], (dv, LANES)))
    kern = functools.partial(_flash_t_kernel, tq=tq, ck=ck, n_comp=n_comp, comp_dk=dk // n_comp,
                             moba_topk=moba_topk, diff_lambda_init=lambda_init)
    scratch = [
        pltpu.VMEM((2 * n_comp, ck, tq), F32),
        pltpu.VMEM((2 * n_comp, ck, tq), BF16),
        pltpu.VMEM((2 * n_comp, 1, tq), F32),
        pltpu.VMEM((n_comp, 1, tq), F32),
        pltpu.VMEM((n_comp, 1, tq), F32),
        pltpu.VMEM((n_comp, dv, tq), F32),
    ]
    return pl.pallas_call(
        kern,
        grid=(batch, heads, nq),
        in_specs=in_specs,
        out_specs=pl.BlockSpec((tq, dv), lambda b, h, i: (b * nq + i, h)),
        out_shape=jax.ShapeDtypeStruct((batch * seq, heads * dv), BF16),
        scratch_shapes=scratch,
        compiler_params=_cparams("parallel", "parallel", "arbitrary"),
        name=name,
    )(*args)


def _swa_kernel(sink_ref, q_ref, k_ref, v_ref, o_ref, *, tq, window, group):
    kh = pl.program_id(1)
    i = pl.program_id(2)
    half = LANES // 2
    n_sub = tq // window
    pairs = group // 2

    def sub(sb, carry):
        r0 = i * tq + sb * window
        k0 = jnp.maximum(r0 - window, 0)
        qrow = pl.multiple_of(sb * window, window)
        kstart = pl.multiple_of(k0, window)
        kk = k_ref[pl.ds(kstart, 2 * window), :]
        vv = v_ref[pl.ds(kstart, 2 * window), :]
        qpos = r0 + lax.broadcasted_iota(jnp.int32, (window, 2 * window), 0)
        kpos = k0 + lax.broadcasted_iota(jnp.int32, (window, 2 * window), 1)
        valid = (kpos <= qpos) & (kpos > qpos - window)
        lane_lo = lax.broadcasted_iota(jnp.int32, (window, LANES), 1) < half
        for pr in range(pairs):
            q2 = q_ref[pl.ds(qrow, window), pr * LANES:(pr + 1) * LANES]
            outs = []
            for e in range(2):
                keep = lane_lo if e == 0 else jnp.logical_not(lane_lo)
                qe = jnp.where(keep, q2, jnp.zeros_like(q2))
                s = jnp.where(valid, _dot_t(qe, kk), NEG_INF)
                sink = sink_ref[kh * group + 2 * pr + e]
                m = jnp.maximum(jnp.max(s, axis=1, keepdims=True), sink)
                p = jnp.exp(s - m)
                denom = jnp.sum(p, axis=1, keepdims=True) + jnp.exp(sink - m)
                p = p / denom
                outs.append(_dot(p.astype(BF16), vv))
            o_pair = jnp.where(lane_lo, outs[0], outs[1])
            o_ref[pl.ds(qrow, window), pr * LANES:(pr + 1) * LANES] = o_pair.astype(o_ref.dtype)
        return carry

    lax.fori_loop(0, n_sub, sub, 0)


def _swa_attention(q, k, v, sinks, *, batch, seq, kv_heads, tq=512):
    tq = min(tq, seq)
    assert seq % tq == 0 and tq % SWA_WINDOW == 0 and seq >= 2 * SWA_WINDOW
    nq = seq // tq
    gw = SWA_GROUP * SWA_HEAD_DIM
    kern = functools.partial(_swa_kernel, tq=tq, window=SWA_WINDOW, group=SWA_GROUP)
    return pl.pallas_call(
        kern,
        grid=(batch, kv_heads, nq),
        in_specs=[
            pl.BlockSpec(memory_space=pltpu.SMEM),
            pl.BlockSpec((tq, gw), lambda b, h, i: (b * nq + i, h)),
            pl.BlockSpec((seq, LANES), lambda b, h, i: (b, h)),
            pl.BlockSpec((seq, LANES), lambda b, h, i: (b, h)),
        ],
        out_specs=pl.BlockSpec((tq, gw), lambda b, h, i: (b * nq + i, h)),
        out_shape=jax.ShapeDtypeStruct((batch * seq, kv_heads * gw), BF16),
        compiler_params=_cparams("parallel", "parallel", "arbitrary"),
        name="swa_attn",
    )(sinks.astype(F32), q, k, v)


def _tile_gain(g, n):
    return jnp.tile(g.astype(F32), n)


def _moba_layer(x, an, wq, wk, wv, gq, gk, wo, *, batch, seq):
    dh, blk = MOBA_HEAD_DIM, MOBA_BLOCK
    heads = wq.shape[1] // dh
    rot = dh // ROT_FRACTION
    rope = _rope_tables(seq, LANES, rot)
    tiles_per_seq = seq // 512
    q_t = _norm_proj_t(x, an, wq.T.astype(BF16), pattern=[("norm", dh, True)], seq=seq, pos_chunk=blk,
                       hg=_tile_gain(gq, heads), rope=rope, rope_half=rot // 2, out_scale=dh ** -0.5,
                       name="moba_q_proj")
    block_id = jnp.arange(seq, dtype=jnp.int32)[:, None] // blk
    onehot = (block_id == jnp.arange(LANES, dtype=jnp.int32)[None, :]).astype(F32)
    k, kmean = _norm_proj(x, an, wk.astype(BF16), pattern=[("norm", dh, False, True)], seq=seq,
                          hg=_tile_gain(gk, heads), rope=rope, rope_half=rot // 2, kmean_rows=blk,
                          pair=(onehot, lambda i, j: (i % tiles_per_seq, 0)), name="moba_k_proj")
    v_t = _norm_proj_t(x, an, wv.T.astype(BF16), pattern=[("copy",)], seq=seq, pos_chunk=VT_CHUNK,
                       name="moba_v_proj")
    kmean = kmean.reshape(kmean.shape[0], kmean.shape[2])
    o = _flash_t(q_t, k, v_t, batch=batch, seq=seq, heads=heads, tq=blk, ck=min(MOBA_KEY_CHUNK, seq), dk=2 * dh,
                 dv=dh, kmean=kmean, name="moba_attn")
    return _matmul_residual(o, wo.astype(BF16), x, name="moba_out_proj")


def _mla_layer(x, an, wq_a, g_qa, wq_b, wkv_a, g_kva, wkv_b, g_qn, g_kn, g_qr, g_kr, wo, *, batch, seq):
    d = x.shape[1]
    nope, rd, vd, pad = MLA_NOPE_DIM, MLA_ROPE_DIM, MLA_V_DIM, MLA_QK_PAD
    q_rank = wq_a.shape[1]
    kv_rank = wkv_a.shape[1] - rd
    heads = wq_b.shape[1] // (nope + rd)
    rope = _rope_tables(seq, LANES, rd)
    zpad = lambda n: jnp.zeros((n,), F32)
    tq = min(512, seq)

    w_a = jnp.concatenate([wq_a, wkv_a, jnp.zeros((d, LANES - rd), F32)], axis=1).astype(BF16)
    n_a = w_a.shape[1]
    chunks_a = n_a // LANES
    hg_a = jnp.concatenate([jnp.ones((q_rank + kv_rank,), F32), g_kr.astype(F32), zpad(LANES - rd)])
    pat_a = [("copy",)] * (chunks_a - 1) + [("norm", rd, False, True)]
    lat = _norm_proj(x, an, w_a, pattern=pat_a, seq=seq, hg=hg_a, rope=rope, rope_half=rd // 2,
                     out_dtype=F32, tn=n_a, name="mla_latent_proj")

    wq_b3 = wq_b.reshape(q_rank, heads, nope + rd)
    wq_p = jnp.concatenate([wq_b3, jnp.zeros((q_rank, heads, pad - nope - rd), F32)], axis=2)
    wq_p = wq_p.reshape(q_rank, heads * pad)
    hg_q = jnp.tile(jnp.concatenate([g_qn.astype(F32), g_qr.astype(F32), zpad(pad - nope - rd)]), heads)
    pat_q = [("norm", nope, False), ("norm", rd, True)]
    q_t = _norm_proj_t(lat, g_qa, wq_p.T.astype(BF16), pattern=pat_q, seq=seq, pos_chunk=tq, xcol=0, hg=hg_q,
                       rope=rope, rope_half=rd // 2, out_scale=(nope + rd) ** -0.5, name="mla_q_proj")

    wkv3 = wkv_b.reshape(kv_rank, heads, nope + vd)
    w_k = wkv3[:, :, :nope].reshape(kv_rank, heads * nope).astype(BF16)
    w_v = wkv3[:, :, nope:].reshape(kv_rank, heads * vd)
    assert q_rank == kv_rank
    kr_col = (q_rank + kv_rank) // LANES
    k = _norm_proj(lat, g_kva, w_k, pattern=[("norm", nope, False, False)], seq=seq, xcol=1,
                   hg=_tile_gain(g_kn, heads), pair=(lat, lambda i, j: (i, kr_col)), name="mla_k_proj")
    v_t = _norm_proj_t(lat, g_kva, w_v.T.astype(BF16), pattern=[("copy",)], seq=seq, pos_chunk=VT_CHUNK, xcol=1,
                       name="mla_v_proj")
    o = _flash_t(q_t, k, v_t, batch=batch, seq=seq, heads=heads, tq=tq, ck=tq, dk=pad, dv=vd, name="mla_attn")
    return _matmul_residual(o, wo.astype(BF16), x, name="mla_out_proj")


def _swa_layer(x, an, wq, wk, wv, gq, gk, sinks, wo, *, batch, seq):
    d = x.shape[1]
    dh = SWA_HEAD_DIM
    hq = wq.shape[1] // dh
    hkv = wk.shape[1] // dh
    rot = dh // ROT_FRACTION
    rope = _rope_tables(seq, dh, rot)

    def dup(w):
        w3 = w.reshape(d, hkv, dh)
        return jnp.concatenate([w3, w3], axis=2).reshape(d, hkv * 2 * dh).astype(BF16)

    q = _norm_proj(x, an, wq.astype(BF16), pattern=[("norm", dh, True, True)], seq=seq, hg=_tile_gain(gq, hq),
                   rope=rope, rope_half=rot // 2, out_scale=dh ** -0.5, name="swa_q_proj")
    k = _norm_proj(x, an, dup(wk), pattern=[("norm", 2 * dh, False, True)], seq=seq,
                   hg=_tile_gain(gk, 2 * hkv), rope=rope, rope_half=rot // 2, name="swa_k_proj")
    v = _norm_proj(x, an, dup(wv), pattern=[("copy",)], seq=seq, name="swa_v_proj")
    o = _swa_attention(q, k, v, sinks, batch=batch, seq=seq, kv_heads=hkv)
    return _matmul_residual(o, wo.astype(BF16), x, name="swa_out_proj")


def _diff_layer(x, an, wq, wk, wv, gq, gk, lq1, lk1, lq2, lk2, g_sub, wo, lambda_init, *, batch, seq):
    dh = DIFF_HEAD_DIM
    heads = wq.shape[1] // (2 * dh)
    rot = dh // ROT_FRACTION
    rope = _rope_tables(seq, LANES, rot)
    tq = min(512, seq)
    q_t = _norm_proj_t(x, an, wq.T.astype(BF16), pattern=[("norm", dh, True)], seq=seq, pos_chunk=tq,
                       hg=_tile_gain(gq, 2 * heads), rope=rope, rope_half=rot // 2, out_scale=dh ** -0.5,
                       name="diff_q_proj")
    k = _norm_proj(x, an, wk.astype(BF16), pattern=[("norm", dh, False, True)], seq=seq,
                   hg=_tile_gain(gk, 2 * heads), rope=rope, rope_half=rot // 2, name="diff_k_proj")
    v_t = _norm_proj_t(x, an, wv.T.astype(BF16), pattern=[("copy",)], seq=seq, pos_chunk=VT_CHUNK,
                       name="diff_v_proj")
    o = _flash_t(q_t, k, v_t, batch=batch, seq=seq, heads=heads, tq=tq, ck=tq, dk=2 * dh, dv=2 * dh, n_comp=2,
                 diff=(lq1, lk1, lq2, lk2, g_sub, lambda_init), name="diff_attn")
    return _matmul_residual(o, wo.astype(BF16), x, name="diff_out_proj")


def _ffn_layer(x, fn, w_gate, w_up, conv_w, conv_b, w_down, *, seq):
    act = _ffn_up(x, fn, w_gate.astype(BF16), w_up.astype(BF16), conv_w, conv_b, seq=seq)
    return _matmul_residual(act, w_down.astype(BF16), x, name="ffn_down")


def kernel(x, attn_norm, ffn_norm, moba_wq, moba_wk, moba_wv, moba_gq, moba_gk, moba_wo, mla_wq_a, mla_g_qa, mla_wq_b, mla_wkv_a, mla_g_kva, mla_wkv_b, mla_g_qn, mla_g_kn, mla_g_qr, mla_g_kr, mla_wo, swa_wq, swa_wk, swa_wv, swa_gq, swa_gk, swa_sinks, swa_wo, diff_wq, diff_wk, diff_wv, diff_gq, diff_gk, diff_lq1, diff_lk1, diff_lq2, diff_lk2, diff_g_sub, diff_wo, ffn_w_gate, ffn_w_up, ffn_conv_w, ffn_conv_b, ffn_w_down):
    batch, seq, d = x.shape
    h = x.reshape(batch * seq, d)
    kw = dict(batch=batch, seq=seq)
    for i in range(DEPTH):
        m, j = i % 4, i // 4
        if m == 0:
            h = _moba_layer(h, attn_norm[i], moba_wq[j], moba_wk[j], moba_wv[j], moba_gq[j], moba_gk[j],
                            moba_wo[j], **kw)
        elif m == 1:
            h = _mla_layer(h, attn_norm[i], mla_wq_a[j], mla_g_qa[j], mla_wq_b[j], mla_wkv_a[j], mla_g_kva[j],
                           mla_wkv_b[j], mla_g_qn[j], mla_g_kn[j], mla_g_qr[j], mla_g_kr[j], mla_wo[j], **kw)
        elif m == 2:
            h = _swa_layer(h, attn_norm[i], swa_wq[j], swa_wk[j], swa_wv[j], swa_gq[j], swa_gk[j],
                           swa_sinks[j], swa_wo[j], **kw)
        else:
            lambda_init = 0.8 - 0.6 * math.exp(-0.3 * i)
            h = _diff_layer(h, attn_norm[i], diff_wq[j], diff_wk[j], diff_wv[j], diff_gq[j], diff_gk[j],
                            diff_lq1[j], diff_lk1[j], diff_lq2[j], diff_lk2[j], diff_g_sub[j], diff_wo[j],
                            lambda_init, **kw)
        h = _ffn_layer(h, ffn_norm[i], ffn_w_gate[i], ffn_w_up[i], ffn_conv_w[i], ffn_conv_b[i],
                       ffn_w_down[i], seq=seq)
    return h.reshape(batch, seq, d)
```

```python
import functools
import math

import jax
import jax.numpy as jnp
from jax import lax
from jax.experimental import pallas as pl
from jax.experimental.pallas import tpu as pltpu

F32 = jnp.float32
BF16 = jnp.bfloat16

LANES = 128
NORM_EPS = 1e-6
NEG_INF = -1e30
ROPE_THETA = 500000.0
ROT_FRACTION = 4
DEPTH = 4

MOBA_HEAD_DIM = 128
MOBA_BLOCK = 256
MOBA_TOPK = 3
MOBA_KEY_CHUNK = 1024
MOBA_QUERY_TILE = 512

MLA_NOPE_DIM = 128
MLA_ROPE_DIM = 64
MLA_V_DIM = 128
MLA_QK_PAD = 256

SWA_HEAD_DIM = 64
SWA_GROUP = 8
SWA_WINDOW = 128

DIFF_HEAD_DIM = 128

CONV_WIDTH = 3
CONV_HALO = 8

VT_CHUNK = 256
SUM_ROWS = 16
LOG2E = math.log2(math.e)
MXU_RESIDENT_BYTES = 8 * 1024 * 1024
VMEM_LIMIT = 56 * 1024 * 1024


def _cparams(*sem):
    return pltpu.CompilerParams(dimension_semantics=sem, vmem_limit_bytes=VMEM_LIMIT)


def _dot(a, b):
    return jnp.dot(a, b, preferred_element_type=F32)


def _dot_t(a, b):
    return lax.dot_general(a, b, (((1,), (1,)), ((), ())), preferred_element_type=F32)


def _rep(x, n):
    return x if n == 1 else jnp.concatenate([x] * n, axis=1)


def _rope_tables(seq, seg, rot):
    half = rot // 2
    inv = ROPE_THETA ** (-jnp.arange(half, dtype=F32) / half)
    ang = jnp.arange(seq, dtype=jnp.int32).astype(F32)[:, None] * inv[None, :]
    cos, sin = jnp.cos(ang), jnp.sin(ang)
    ones = jnp.ones((seq, seg - rot), F32)
    zeros_h = jnp.zeros((seq, half), F32)
    zeros_r = jnp.zeros((seq, seg - rot), F32)
    c = jnp.concatenate([cos, cos, ones], axis=1)
    s1 = jnp.concatenate([-sin, zeros_h, zeros_r], axis=1)
    s2 = jnp.concatenate([zeros_h, sin, zeros_r], axis=1)
    reps = LANES // seg
    return tuple(jnp.tile(t, (1, reps)) for t in (c, s1, s2))


def _rms_rows_to_bf16(x_ref, g_ref, xn_ref):
    x = x_ref[...]
    ms = jnp.mean(x * x, axis=-1, keepdims=True)
    xn_ref[...] = (x * lax.rsqrt(ms + NORM_EPS) * g_ref[...]).astype(BF16)


def _proj_kernel(*refs, pattern, rope_half, out_scale, n_chunks, has_rope, has_hg, kmean_rows, has_pair):
    it = iter(refs)
    x_ref, g_ref, w_ref = next(it), next(it), next(it)
    hg_ref = next(it) if has_hg else None
    if has_rope:
        c_ref, s1_ref, s2_ref = next(it), next(it), next(it)
    pair_ref = next(it) if has_pair else None
    o_ref = next(it)
    km_ref = next(it) if kmean_rows else None
    xn_ref = next(it)

    @pl.when(pl.program_id(1) == 0)
    def _():
        _rms_rows_to_bf16(x_ref, g_ref, xn_ref)

    acc = _dot(xn_ref[...], w_ref[...])
    for c in range(n_chunks):
        mode = pattern[c % len(pattern)]
        blk = acc[:, c * LANES:(c + 1) * LANES]
        if mode[0] == "norm":
            _, width, two, rope = mode
            sq = blk * blk
            if two:
                lo = lax.broadcasted_iota(jnp.int32, blk.shape, 1) < (LANES // 2)
                s_lo = jnp.sum(jnp.where(lo, sq, 0.0), axis=1, keepdims=True)
                s_hi = jnp.sum(jnp.where(lo, 0.0, sq), axis=1, keepdims=True)
                r = jnp.where(lo, lax.rsqrt(s_lo * (1.0 / width) + NORM_EPS),
                              lax.rsqrt(s_hi * (1.0 / width) + NORM_EPS))
            else:
                r = lax.rsqrt(jnp.sum(sq, axis=1, keepdims=True) * (1.0 / width) + NORM_EPS)
            blk = blk * r * hg_ref[:, c * LANES:(c + 1) * LANES]
            if rope:
                blk = (blk * c_ref[...]
                       + pltpu.roll(blk, LANES - rope_half, 1) * s1_ref[...]
                       + pltpu.roll(blk, rope_half, 1) * s2_ref[...])
            if out_scale != 1.0:
                blk = blk * out_scale
        if has_pair:
            o_ref[:, (2 * c) * LANES:(2 * c + 1) * LANES] = blk.astype(o_ref.dtype)
            o_ref[:, (2 * c + 1) * LANES:(2 * c + 2) * LANES] = pair_ref[...].astype(o_ref.dtype)
        else:
            o_ref[:, c * LANES:(c + 1) * LANES] = blk.astype(o_ref.dtype)
        if kmean_rows:
            for r_i in range(blk.shape[0] // kmean_rows):
                part = blk[r_i * kmean_rows:(r_i + 1) * kmean_rows, :]
                km_ref[r_i:r_i + 1, :, c * LANES:(c + 1) * LANES] = (
                    jnp.sum(part, axis=0, keepdims=True) * (1.0 / kmean_rows))[None]


def _norm_proj(x, gain, w, *, pattern, seq, name, xcol=0, hg=None, rope=None, rope_half=0, out_scale=1.0,
               out_dtype=None, kmean_rows=0, pair=None, tm=512):
    n_rows = x.shape[0]
    k_dim, n_out = w.shape
    assert k_dim * n_out * w.dtype.itemsize <= MXU_RESIDENT_BYTES
    tn = n_out
    out_dtype = out_dtype or BF16
    assert n_rows % tm == 0 and n_out % tn == 0 and seq % tm == 0
    n_chunks = tn // LANES
    assert n_chunks % len(pattern) == 0
    tiles_per_seq = seq // tm
    in_specs = [
        pl.BlockSpec((tm, k_dim), lambda i, j: (i, xcol)),
        pl.BlockSpec((1, k_dim), lambda i, j: (0, 0)),
        pl.BlockSpec((k_dim, tn), lambda i, j: (0, j)),
    ]
    args = [x, gain.reshape(1, k_dim).astype(F32), w]
    if hg is not None:
        in_specs.append(pl.BlockSpec((1, tn), lambda i, j: (0, j)))
        args.append(hg.reshape(1, n_out).astype(F32))
    if rope is not None:
        for t in rope:
            in_specs.append(pl.BlockSpec((tm, LANES), lambda i, j: (i % tiles_per_seq, 0)))
            args.append(t)
    out_mult = 1
    if pair is not None:
        pair_arr, pair_map = pair
        in_specs.append(pl.BlockSpec((tm, LANES), pair_map))
        args.append(pair_arr)
        out_mult = 2
    out_shape = [jax.ShapeDtypeStruct((n_rows, n_out * out_mult), out_dtype)]
    out_specs = [pl.BlockSpec((tm, tn * out_mult), lambda i, j: (i, j))]
    if kmean_rows:
        out_shape.append(jax.ShapeDtypeStruct((n_rows // kmean_rows, 1, n_out), F32))
        out_specs.append(pl.BlockSpec((tm // kmean_rows, 1, tn), lambda i, j: (i, 0, j)))
    kern = functools.partial(
        _proj_kernel, pattern=tuple(pattern), rope_half=rope_half, out_scale=out_scale, n_chunks=n_chunks,
        has_rope=rope is not None, has_hg=hg is not None, kmean_rows=kmean_rows, has_pair=pair is not None)
    res = pl.pallas_call(
        kern,
        grid=(n_rows // tm, n_out // tn),
        in_specs=in_specs,
        out_specs=out_specs,
        out_shape=out_shape,
        scratch_shapes=[pltpu.VMEM((tm, k_dim), BF16)],
        compiler_params=_cparams("parallel", "arbitrary"),
        name=name,
    )(*args)
    return res if kmean_rows else res[0]


def _proj_t_kernel(*refs, pattern, rope_half, out_scale, n_chunks, has_rope, has_hg, pos_chunk):
    it = iter(refs)
    x_ref, g_ref, w_ref = next(it), next(it), next(it)
    hg_ref = next(it) if has_hg else None
    if has_rope:
        c_ref, s1_ref, s2_ref = next(it), next(it), next(it)
    o_ref = next(it)
    xn_ref = next(it)

    @pl.when(pl.program_id(1) == 0)
    def _():
        _rms_rows_to_bf16(x_ref, g_ref, xn_ref)

    acc = _dot_t(w_ref[...], xn_ref[...])
    tm = acc.shape[1]
    for c in range(n_chunks):
        mode = pattern[c % len(pattern)]
        blk = acc[c * LANES:(c + 1) * LANES, :]
        if mode[0] == "norm":
            _, width, rope = mode
            r = lax.rsqrt(jnp.sum(blk * blk, axis=0, keepdims=True) * (1.0 / width) + NORM_EPS)
            blk = blk * r * _rep(hg_ref[c * LANES:(c + 1) * LANES, :], tm // LANES)
            if rope:
                up = jnp.concatenate([blk[rope_half:, :], blk[:rope_half, :]], axis=0)
                dn = jnp.concatenate([blk[-rope_half:, :], blk[:-rope_half, :]], axis=0)
                blk = blk * c_ref[...] + up * s1_ref[...] + dn * s2_ref[...]
            if out_scale != 1.0:
                blk = blk * out_scale
        for pc in range(tm // pos_chunk):
            o_ref[pc, c * LANES:(c + 1) * LANES, :] = blk[:, pc * pos_chunk:(pc + 1) * pos_chunk].astype(o_ref.dtype)


def _norm_proj_t(x, gain, w_t, *, pattern, seq, name, pos_chunk, xcol=0, hg=None, rope=None, rope_half=0,
                 out_scale=1.0, tm=512):
    n_rows = x.shape[0]
    n_out, k_dim = w_t.shape
    assert k_dim * n_out * w_t.dtype.itemsize <= MXU_RESIDENT_BYTES
    tn = n_out
    assert n_rows % tm == 0 and n_out % tn == 0 and seq % tm == 0 and tm % pos_chunk == 0
    n_chunks = tn // LANES
    assert n_chunks % len(pattern) == 0
    tiles_per_seq = seq // tm
    in_specs = [
        pl.BlockSpec((tm, k_dim), lambda i, j: (i, xcol)),
        pl.BlockSpec((1, k_dim), lambda i, j: (0, 0)),
        pl.BlockSpec((tn, k_dim), lambda i, j: (j, 0)),
    ]
    args = [x, gain.reshape(1, k_dim).astype(F32), w_t]
    if hg is not None:
        in_specs.append(pl.BlockSpec((tn, LANES), lambda i, j: (j, 0)))
        args.append(jnp.broadcast_to(hg.astype(F32)[:, None], (n_out, LANES)))
    if rope is not None:
        for t in rope:
            in_specs.append(pl.BlockSpec((LANES, tm), lambda i, j: (0, i % tiles_per_seq)))
            args.append(t.T)
    kern = functools.partial(
        _proj_t_kernel, pattern=tuple(pattern), rope_half=rope_half, out_scale=out_scale, n_chunks=n_chunks,
        has_rope=rope is not None, has_hg=hg is not None, pos_chunk=pos_chunk)
    per_tile = tm // pos_chunk
    return pl.pallas_call(
        kern,
        grid=(n_rows // tm, n_out // tn),
        in_specs=in_specs,
        out_specs=pl.BlockSpec((per_tile, tn, pos_chunk), lambda i, j: (i, j, 0)),
        out_shape=jax.ShapeDtypeStruct((n_rows // pos_chunk, n_out, pos_chunk), BF16),
        scratch_shapes=[pltpu.VMEM((tm, k_dim), BF16)],
        compiler_params=_cparams("parallel", "arbitrary"),
        name=name,
    )(*args)


def _mm_res_kernel(a_ref, w_ref, r_ref, o_ref):
    o_ref[...] = r_ref[...] + _dot(a_ref[...], w_ref[...])


def _matmul_residual(a, w, res, *, name):
    n_rows, k_dim = a.shape
    n_out = w.shape[1]
    if k_dim * n_out * w.dtype.itemsize <= MXU_RESIDENT_BYTES:
        tm, tn = 512, n_out
    else:
        tm, tn = 1024, 512
    tm = min(tm, n_rows)
    assert n_rows % tm == 0 and n_out % tn == 0
    return pl.pallas_call(
        _mm_res_kernel,
        grid=(n_rows // tm, n_out // tn),
        in_specs=[
            pl.BlockSpec((tm, k_dim), lambda i, j: (i, 0)),
            pl.BlockSpec((k_dim, tn), lambda i, j: (0, j)),
            pl.BlockSpec((tm, tn), lambda i, j: (i, j)),
        ],
        out_specs=pl.BlockSpec((tm, tn), lambda i, j: (i, j)),
        out_shape=jax.ShapeDtypeStruct((n_rows, n_out), F32),
        compiler_params=_cparams("parallel", "parallel"),
        name=name,
    )(a, w, res)


def _ffn_up_kernel(x_ref, xh_ref, g_ref, wg_ref, wu_ref, cw_ref, cb_ref, o_ref, xn_ref, xhn_ref, gs_ref,
                   *, tm, tiles_per_seq):
    i = pl.program_id(0)

    @pl.when(pl.program_id(1) == 0)
    def _():
        _rms_rows_to_bf16(x_ref, g_ref, xn_ref)
        _rms_rows_to_bf16(xh_ref, g_ref, xhn_ref)

    wg = wg_ref[...]
    g = _dot(xn_ref[...], wg)
    gh = _dot(xhn_ref[...], wg)
    seq_start = (i % tiles_per_seq) == 0
    gs_ref[0:CONV_HALO, :] = jnp.where(seq_start, 0.0, gh)
    gs_ref[CONV_HALO:CONV_HALO + tm, :] = g
    y = (cw_ref[0:1, :] * gs_ref[CONV_HALO - 2:CONV_HALO - 2 + tm, :]
         + cw_ref[1:2, :] * gs_ref[CONV_HALO - 1:CONV_HALO - 1 + tm, :]
         + cw_ref[2:3, :] * g + cb_ref[...])
    u = _dot(xn_ref[...], wu_ref[...])
    act = y * (1.0 / (1.0 + jnp.exp(-y))) * u
    o_ref[...] = act.astype(o_ref.dtype)


def _ffn_up(x, gain, wg, wu, conv_w, conv_b, *, seq, tm=512, tf=512):
    n_rows, d = x.shape
    f = wg.shape[1]
    assert n_rows % tm == 0 and f % tf == 0 and seq % tm == 0 and tm % CONV_HALO == 0
    tiles_per_seq = seq // tm
    halo_blocks = tm // CONV_HALO
    kern = functools.partial(_ffn_up_kernel, tm=tm, tiles_per_seq=tiles_per_seq)
    return pl.pallas_call(
        kern,
        grid=(n_rows // tm, f // tf),
        in_specs=[
            pl.BlockSpec((tm, d), lambda i, j: (i, 0)),
            pl.BlockSpec((CONV_HALO, d), lambda i, j: (jnp.maximum(i * halo_blocks - 1, 0), 0)),
            pl.BlockSpec((1, d), lambda i, j: (0, 0)),
            pl.BlockSpec((d, tf), lambda i, j: (0, j)),
            pl.BlockSpec((d, tf), lambda i, j: (0, j)),
            pl.BlockSpec((CONV_WIDTH, tf), lambda i, j: (0, j)),
            pl.BlockSpec((1, tf), lambda i, j: (0, j)),
        ],
        out_specs=pl.BlockSpec((tm, tf), lambda i, j: (i, j)),
        out_shape=jax.ShapeDtypeStruct((n_rows, f), BF16),
        scratch_shapes=[pltpu.VMEM((tm, d), BF16), pltpu.VMEM((CONV_HALO, d), BF16),
                        pltpu.VMEM((tm + CONV_HALO, tf), F32)],
        compiler_params=_cparams("parallel", "arbitrary"),
        name="ffn_up",
    )(x, x, gain.reshape(1, d).astype(F32), wg, wu, conv_w.astype(F32), conv_b.reshape(1, f).astype(F32))


def _flash_t_kernel(*refs, tq, ck, n_comp, comp_dk, moba_topk, diff_lambda_init):
    it = iter(refs)
    q_ref, k_ref, v_ref = next(it), next(it), next(it)
    km_ref = next(it) if moba_topk else None
    if diff_lambda_init is not None:
        lq1_ref, lk1_ref, lq2_ref, lk2_ref, gs_ref = (next(it) for _ in range(5))
    o_ref = next(it)
    s_ref, p_ref, a_ref, m_ref, acc_ref = (next(it) for _ in range(5))
    dv = acc_ref.shape[1] - SUM_ROWS

    i = pl.program_id(2)
    qt = q_ref[0]
    m_ref[...] = jnp.full(m_ref.shape, NEG_INF, F32)
    acc_ref[...] = jnp.zeros(acc_ref.shape, F32)

    if moba_topk:
        nb = km_ref.shape[0]
        km = km_ref[...]
        km_hi = km.astype(BF16)
        km_lo = (km - km_hi.astype(F32)).astype(BF16)
        gate = _dot(km_hi, qt) + _dot(km_lo, qt)
        row = lax.broadcasted_iota(jnp.int32, gate.shape, 0)
        own = i * (tq // MOBA_BLOCK) + lax.broadcasted_iota(jnp.int32, gate.shape, 1) // MOBA_BLOCK
        neg = jnp.float32(-jnp.inf)
        gate = jnp.where(row < own, gate, neg)
        allowed = row == own
        for _ in range(moba_topk):
            mx = jnp.max(gate, axis=0, keepdims=True)
            first = jnp.min(jnp.where(gate == mx, row, nb), axis=0, keepdims=True)
            pick = (row == first) & (mx > neg)
            allowed = allowed | pick
            gate = jnp.where(pick, neg, gate)
        bias = jnp.where(allowed, 0.0, NEG_INF)
        bias = jnp.concatenate([bias, jnp.zeros((LANES - nb, tq), F32)], axis=0).astype(BF16)
        q_parts = [jnp.concatenate([qt, bias], axis=0)]
    else:
        q_parts = [qt[c * comp_dk:(c + 1) * comp_dk, :] for c in range(n_comp)]

    slabs = ck // VT_CHUNK

    def qk(chunk, slot):
        start = pl.multiple_of(chunk * ck, ck)
        kc = k_ref[pl.ds(start, ck), :]
        for c in range(n_comp):
            s_ref[slot * n_comp + c] = _dot(kc[:, c * comp_dk:(c + 1) * comp_dk], q_parts[c])

    def sm(chunk, slot, masked):
        for c in range(n_comp):
            s = s_ref[slot * n_comp + c]
            if masked:
                key = chunk * ck + lax.broadcasted_iota(jnp.int32, s.shape, 0)
                qry = i * tq + lax.broadcasted_iota(jnp.int32, s.shape, 1)
                s = jnp.where(key <= qry, s, NEG_INF)
            m_prev = m_ref[c]
            m_next = jnp.maximum(m_prev, jnp.max(s, axis=0, keepdims=True))
            m_ref[c] = m_next
            a_ref[slot * n_comp + c] = jnp.exp2(m_prev - m_next)
            p_ref[slot * n_comp + c] = jnp.exp2(s - m_next).astype(BF16)

    ones = jnp.ones((SUM_ROWS, ck), BF16)

    def pv(chunk, slot):
        vt = [v_ref[chunk * slabs + t] for t in range(slabs)]
        vt = vt[0] if slabs == 1 else jnp.concatenate(vt, axis=1)
        vt = jnp.concatenate([vt, ones], axis=0)
        for c in range(n_comp):
            acc_ref[c] = acc_ref[c] * a_ref[slot * n_comp + c] + _dot(vt, p_ref[slot * n_comp + c])

    n = (i * tq) // ck + 1
    qk(0, 0)

    @pl.when(n >= 2)
    def _():
        sm(0, 0, False)
        qk(1, 1)

    def pair_body(u, carry):
        c0 = 2 * u
        qk(c0 + 2, 0)
        pv(c0, 0)
        sm(c0 + 1, 1, False)
        qk(c0 + 3, 1)
        pv(c0 + 1, 1)
        sm(c0 + 2, 0, False)
        return carry

    pairs = jnp.maximum(n - 2, 0) // 2
    lax.fori_loop(0, pairs, pair_body, 0)

    @pl.when((n % 2 == 1) & (n >= 3))
    def _():
        qk(n - 1, 0)
        pv(n - 3, 0)
        sm(n - 2, 1, False)
        pv(n - 2, 1)

    @pl.when(n % 2 == 1)
    def _():
        sm(n - 1, 0, True)
        pv(n - 1, 0)

    @pl.when(n % 2 == 0)
    def _():
        pv(n - 2, 0)
        sm(n - 1, 1, True)
        pv(n - 1, 1)

    outs = [acc_ref[c, 0:dv, :] / acc_ref[c, dv:dv + 1, :] for c in range(n_comp)]
    if diff_lambda_init is None:
        o = outs[0]
    else:
        lam = (jnp.exp(jnp.sum(lq1_ref[...] * lk1_ref[...], axis=1, keepdims=True))
               - jnp.exp(jnp.sum(lq2_ref[...] * lk2_ref[...], axis=1, keepdims=True)) + diff_lambda_init)
        o = outs[0] - lam * outs[1]
        ms = jnp.mean(o * o, axis=0, keepdims=True)
        o = o * lax.rsqrt(ms + NORM_EPS) * _rep(gs_ref[...], tq // LANES) * (1.0 - diff_lambda_init)
    o_ref[...] = o.T.astype(o_ref.dtype)


def _flash_t(q_t, k, v_t, *, batch, seq, heads, tq, ck, dk, dv, name, n_comp=1, kmean=None, diff=None):
    assert seq % ck == 0 and ck % tq == 0 and tq % VT_CHUNK == 0
    nq = seq // tq
    dq = q_t.shape[1] // heads
    assert q_t.shape == (batch * nq, heads * dq, tq) and k.shape == (batch * seq, heads * dk)
    assert v_t.shape == (batch * seq // VT_CHUNK, heads * dv, VT_CHUNK)
    slabs_per_seq = seq // VT_CHUNK
    in_specs = [
        pl.BlockSpec((1, dq, tq), lambda b, h, i: (b * nq + i, h, 0)),
        pl.BlockSpec((seq, dk), lambda b, h, i: (b, h)),
        pl.BlockSpec((slabs_per_seq, dv, VT_CHUNK), lambda b, h, i: (b, h, 0)),
    ]
    args = [q_t, k, v_t]
    moba_topk = 0
    if kmean is not None:
        nb = seq // MOBA_BLOCK
        assert tq % MOBA_BLOCK == 0 and nb <= LANES and dq + LANES == dk
        moba_topk = min(MOBA_TOPK, nb)
        in_specs.append(pl.BlockSpec((nb, dq), lambda b, h, i: (b, h)))
        args.append(kmean)
    lambda_init = None
    if diff is not None:
        lq1, lk1, lq2, lk2, g_sub, lambda_init = diff
        for a in (lq1, lk1, lq2, lk2):
            in_specs.append(pl.BlockSpec((1, a.shape[0]), lambda b, h, i: (0, 0)))
            args.append(a.reshape(1, -1).astype(F32))
        in_specs.append(pl.BlockSpec((dv, LANES), lambda b, h, i: (0, 0)))
        args.append(jnp.broadcast_to(g_sub.astype(F32)[:, None], (dv, LANES)))
    kern = functools.partial(_flash_t_kernel, tq=tq, ck=ck, n_comp=n_comp, comp_dk=dk // n_comp,
                             moba_topk=moba_topk, diff_lambda_init=lambda_init)
    scratch = [
        pltpu.VMEM((2 * n_comp, ck, tq), F32),
        pltpu.VMEM((2 * n_comp, ck, tq), BF16),
        pltpu.VMEM((2 * n_comp, 1, tq), F32),
        pltpu.VMEM((n_comp, 1, tq), F32),
        pltpu.VMEM((n_comp, dv + SUM_ROWS, tq), F32),
    ]
    return pl.pallas_call(
        kern,
        grid=(batch, heads, nq),
        in_specs=in_specs,
        out_specs=pl.BlockSpec((tq, dv), lambda b, h, i: (b * nq + i, h)),
        out_shape=jax.ShapeDtypeStruct((batch * seq, heads * dv), BF16),
        scratch_shapes=scratch,
        compiler_params=_cparams("parallel", "parallel", "arbitrary"),
        name=name,
    )(*args)


def _swa_kernel(sink_ref, q_ref, k_ref, v_ref, o_ref, *, tq, window, group):
    kh = pl.program_id(1)
    i = pl.program_id(2)
    half = LANES // 2
    n_sub = tq // window
    pairs = group // 2

    def sub(sb, carry):
        r0 = i * tq + sb * window
        k0 = jnp.maximum(r0 - window, 0)
        qrow = pl.multiple_of(sb * window, window)
        kstart = pl.multiple_of(k0, window)
        kk = k_ref[pl.ds(kstart, 2 * window), :]
        vv = v_ref[pl.ds(kstart, 2 * window), :]
        qpos = r0 + lax.broadcasted_iota(jnp.int32, (window, 2 * window), 0)
        kpos = k0 + lax.broadcasted_iota(jnp.int32, (window, 2 * window), 1)
        valid = (kpos <= qpos) & (kpos > qpos - window)
        lane_lo = lax.broadcasted_iota(jnp.int32, (window, LANES), 1) < half
        for pr in range(pairs):
            q2 = q_ref[pl.ds(qrow, window), pr * LANES:(pr + 1) * LANES]
            outs = []
            for e in range(2):
                keep = lane_lo if e == 0 else jnp.logical_not(lane_lo)
                qe = jnp.where(keep, q2, jnp.zeros_like(q2))
                s = jnp.where(valid, _dot_t(qe, kk), NEG_INF)
                sink = sink_ref[kh * group + 2 * pr + e]
                m = jnp.maximum(jnp.max(s, axis=1, keepdims=True), sink)
                p = jnp.exp(s - m)
                denom = jnp.sum(p, axis=1, keepdims=True) + jnp.exp(sink - m)
                p = p / denom
                outs.append(_dot(p.astype(BF16), vv))
            o_pair = jnp.where(lane_lo, outs[0], outs[1])
            o_ref[pl.ds(qrow, window), pr * LANES:(pr + 1) * LANES] = o_pair.astype(o_ref.dtype)
        return carry

    lax.fori_loop(0, n_sub, sub, 0)


def _swa_attention(q, k, v, sinks, *, batch, seq, kv_heads, tq=512):
    tq = min(tq, seq)
    assert seq % tq == 0 and tq % SWA_WINDOW == 0 and seq >= 2 * SWA_WINDOW
    nq = seq // tq
    gw = SWA_GROUP * SWA_HEAD_DIM
    kern = functools.partial(_swa_kernel, tq=tq, window=SWA_WINDOW, group=SWA_GROUP)
    return pl.pallas_call(
        kern,
        grid=(batch, kv_heads, nq),
        in_specs=[
            pl.BlockSpec(memory_space=pltpu.SMEM),
            pl.BlockSpec((tq, gw), lambda b, h, i: (b * nq + i, h)),
            pl.BlockSpec((seq, LANES), lambda b, h, i: (b, h)),
            pl.BlockSpec((seq, LANES), lambda b, h, i: (b, h)),
        ],
        out_specs=pl.BlockSpec((tq, gw), lambda b, h, i: (b * nq + i, h)),
        out_shape=jax.ShapeDtypeStruct((batch * seq, kv_heads * gw), BF16),
        compiler_params=_cparams("parallel", "parallel", "arbitrary"),
        name="swa_attn",
    )(sinks.astype(F32), q, k, v)


def _tile_gain(g, n):
    return jnp.tile(g.astype(F32), n)


def _moba_layer(x, an, wq, wk, wv, gq, gk, wo, *, batch, seq):
    dh, blk = MOBA_HEAD_DIM, MOBA_BLOCK
    heads = wq.shape[1] // dh
    rot = dh // ROT_FRACTION
    rope = _rope_tables(seq, LANES, rot)
    tiles_per_seq = seq // 512
    tq = min(MOBA_QUERY_TILE, seq)
    q_t = _norm_proj_t(x, an, wq.T.astype(BF16), pattern=[("norm", dh, True)], seq=seq, pos_chunk=tq,
                       hg=_tile_gain(gq, heads), rope=rope, rope_half=rot // 2, out_scale=dh ** -0.5 * LOG2E,
                       name="moba_q_proj")
    block_id = jnp.arange(seq, dtype=jnp.int32)[:, None] // blk
    onehot = (block_id == jnp.arange(LANES, dtype=jnp.int32)[None, :]).astype(F32)
    k, kmean = _norm_proj(x, an, wk.astype(BF16), pattern=[("norm", dh, False, True)], seq=seq,
                          hg=_tile_gain(gk, heads), rope=rope, rope_half=rot // 2, kmean_rows=blk,
                          pair=(onehot, lambda i, j: (i % tiles_per_seq, 0)), name="moba_k_proj")
    v_t = _norm_proj_t(x, an, wv.T.astype(BF16), pattern=[("copy",)], seq=seq, pos_chunk=VT_CHUNK,
                       name="moba_v_proj")
    kmean = kmean.reshape(kmean.shape[0], kmean.shape[2])
    o = _flash_t(q_t, k, v_t, batch=batch, seq=seq, heads=heads, tq=tq, ck=min(MOBA_KEY_CHUNK, seq), dk=2 * dh,
                 dv=dh, kmean=kmean, name="moba_attn")
    return _matmul_residual(o, wo.astype(BF16), x, name="moba_out_proj")


def _mla_layer(x, an, wq_a, g_qa, wq_b, wkv_a, g_kva, wkv_b, g_qn, g_kn, g_qr, g_kr, wo, *, batch, seq):
    d = x.shape[1]
    nope, rd, vd, pad = MLA_NOPE_DIM, MLA_ROPE_DIM, MLA_V_DIM, MLA_QK_PAD
    q_rank = wq_a.shape[1]
    kv_rank = wkv_a.shape[1] - rd
    heads = wq_b.shape[1] // (nope + rd)
    rope = _rope_tables(seq, LANES, rd)
    zpad = lambda n: jnp.zeros((n,), F32)
    tq = min(512, seq)

    w_a = jnp.concatenate([wq_a, wkv_a, jnp.zeros((d, LANES - rd), F32)], axis=1).astype(BF16)
    n_a = w_a.shape[1]
    chunks_a = n_a // LANES
    hg_a = jnp.concatenate([jnp.ones((q_rank + kv_rank,), F32), g_kr.astype(F32), zpad(LANES - rd)])
    pat_a = [("copy",)] * (chunks_a - 1) + [("norm", rd, False, True)]
    lat = _norm_proj(x, an, w_a, pattern=pat_a, seq=seq, hg=hg_a, rope=rope, rope_half=rd // 2,
                     out_dtype=F32, name="mla_latent_proj")

    wq_b3 = wq_b.reshape(q_rank, heads, nope + rd)
    wq_p = jnp.concatenate([wq_b3, jnp.zeros((q_rank, heads, pad - nope - rd), F32)], axis=2)
    wq_p = wq_p.reshape(q_rank, heads * pad)
    hg_q = jnp.tile(jnp.concatenate([g_qn.astype(F32), g_qr.astype(F32), zpad(pad - nope - rd)]), heads)
    pat_q = [("norm", nope, False), ("norm", rd, True)]
    q_t = _norm_proj_t(lat, g_qa, wq_p.T.astype(BF16), pattern=pat_q, seq=seq, pos_chunk=tq, xcol=0, hg=hg_q,
                       rope=rope, rope_half=rd // 2, out_scale=(nope + rd) ** -0.5 * LOG2E, name="mla_q_proj")

    wkv3 = wkv_b.reshape(kv_rank, heads, nope + vd)
    w_k = wkv3[:, :, :nope].reshape(kv_rank, heads * nope).astype(BF16)
    w_v = wkv3[:, :, nope:].reshape(kv_rank, heads * vd)
    assert q_rank == kv_rank
    kr_col = (q_rank + kv_rank) // LANES
    k = _norm_proj(lat, g_kva, w_k, pattern=[("norm", nope, False, False)], seq=seq, xcol=1,
                   hg=_tile_gain(g_kn, heads), pair=(lat, lambda i, j: (i, kr_col)), name="mla_k_proj")
    v_t = _norm_proj_t(lat, g_kva, w_v.T.astype(BF16), pattern=[("copy",)], seq=seq, pos_chunk=VT_CHUNK, xcol=1,
                       name="mla_v_proj")
    o = _flash_t(q_t, k, v_t, batch=batch, seq=seq, heads=heads, tq=tq, ck=tq, dk=pad, dv=vd, name="mla_attn")
    return _matmul_residual(o, wo.astype(BF16), x, name="mla_out_proj")


def _swa_layer(x, an, wq, wk, wv, gq, gk, sinks, wo, *, batch, seq):
    d = x.shape[1]
    dh = SWA_HEAD_DIM
    hq = wq.shape[1] // dh
    hkv = wk.shape[1] // dh
    rot = dh // ROT_FRACTION
    rope = _rope_tables(seq, dh, rot)

    def dup(w):
        w3 = w.reshape(d, hkv, dh)
        return jnp.concatenate([w3, w3], axis=2).reshape(d, hkv * 2 * dh).astype(BF16)

    q = _norm_proj(x, an, wq.astype(BF16), pattern=[("norm", dh, True, True)], seq=seq, hg=_tile_gain(gq, hq),
                   rope=rope, rope_half=rot // 2, out_scale=dh ** -0.5, name="swa_q_proj")
    k = _norm_proj(x, an, dup(wk), pattern=[("norm", 2 * dh, False, True)], seq=seq,
                   hg=_tile_gain(gk, 2 * hkv), rope=rope, rope_half=rot // 2, name="swa_k_proj")
    v = _norm_proj(x, an, dup(wv), pattern=[("copy",)], seq=seq, name="swa_v_proj")
    o = _swa_attention(q, k, v, sinks, batch=batch, seq=seq, kv_heads=hkv)
    return _matmul_residual(o, wo.astype(BF16), x, name="swa_out_proj")


def _diff_layer(x, an, wq, wk, wv, gq, gk, lq1, lk1, lq2, lk2, g_sub, wo, lambda_init, *, batch, seq):
    dh = DIFF_HEAD_DIM
    heads = wq.shape[1] // (2 * dh)
    rot = dh // ROT_FRACTION
    rope = _rope_tables(seq, LANES, rot)
    tq = min(512, seq)
    q_t = _norm_proj_t(x, an, wq.T.astype(BF16), pattern=[("norm", dh, True)], seq=seq, pos_chunk=tq,
                       hg=_tile_gain(gq, 2 * heads), rope=rope, rope_half=rot // 2, out_scale=dh ** -0.5 * LOG2E,
                       name="diff_q_proj")
    k = _norm_proj(x, an, wk.astype(BF16), pattern=[("norm", dh, False, True)], seq=seq,
                   hg=_tile_gain(gk, 2 * heads), rope=rope, rope_half=rot // 2, name="diff_k_proj")
    v_t = _norm_proj_t(x, an, wv.T.astype(BF16), pattern=[("copy",)], seq=seq, pos_chunk=VT_CHUNK,
                       name="diff_v_proj")
    o = _flash_t(q_t, k, v_t, batch=batch, seq=seq, heads=heads, tq=tq, ck=tq, dk=2 * dh, dv=2 * dh, n_comp=2,
                 diff=(lq1, lk1, lq2, lk2, g_sub, lambda_init), name="diff_attn")
    return _matmul_residual(o, wo.astype(BF16), x, name="diff_out_proj")


def _ffn_layer(x, fn, w_gate, w_up, conv_w, conv_b, w_down, *, seq):
    act = _ffn_up(x, fn, w_gate.astype(BF16), w_up.astype(BF16), conv_w, conv_b, seq=seq)
    return _matmul_residual(act, w_down.astype(BF16), x, name="ffn_down")


def kernel(x, attn_norm, ffn_norm, moba_wq, moba_wk, moba_wv, moba_gq, moba_gk, moba_wo, mla_wq_a, mla_g_qa, mla_wq_b, mla_wkv_a, mla_g_kva, mla_wkv_b, mla_g_qn, mla_g_kn, mla_g_qr, mla_g_kr, mla_wo, swa_wq, swa_wk, swa_wv, swa_gq, swa_gk, swa_sinks, swa_wo, diff_wq, diff_wk, diff_wv, diff_gq, diff_gk, diff_lq1, diff_lk1, diff_lq2, diff_lk2, diff_g_sub, diff_wo, ffn_w_gate, ffn_w_up, ffn_conv_w, ffn_conv_b, ffn_w_down):
    batch, seq, d = x.shape
    h = x.reshape(batch * seq, d)
    kw = dict(batch=batch, seq=seq)
    for i in range(DEPTH):
        m, j = i % 4, i // 4
        if m == 0:
            h = _moba_layer(h, attn_norm[i], moba_wq[j], moba_wk[j], moba_wv[j], moba_gq[j], moba_gk[j],
                            moba_wo[j], **kw)
        elif m == 1:
            h = _mla_layer(h, attn_norm[i], mla_wq_a[j], mla_g_qa[j], mla_wq_b[j], mla_wkv_a[j], mla_g_kva[j],
                           mla_wkv_b[j], mla_g_qn[j], mla_g_kn[j], mla_g_qr[j], mla_g_kr[j], mla_wo[j], **kw)
        elif m == 2:
            h = _swa_layer(h, attn_norm[i], swa_wq[j], swa_wk[j], swa_wv[j], swa_gq[j], swa_gk[j],
                           swa_sinks[j], swa_wo[j], **kw)
        else:
            lambda_init = 0.8 - 0.6 * math.exp(-0.3 * i)
            h = _diff_layer(h, attn_norm[i], diff_wq[j], diff_wk[j], diff_wv[j], diff_gq[j], diff_gk[j],
                            diff_lq1[j], diff_lk1[j], diff_lq2[j], diff_lk2[j], diff_g_sub[j], diff_wo[j],
                            lambda_init, **kw)
        h = _ffn_layer(h, ffn_norm[i], ffn_w_gate[i], ffn_w_up[i], ffn_conv_w[i], ffn_conv_b[i],
                       ffn_w_down[i], seq=seq)
    return h.reshape(batch, seq, d)
```

```python
import functools
import math

import jax
import jax.numpy as jnp
from jax import lax
from jax.experimental import pallas as pl
from jax.experimental.pallas import tpu as pltpu

F32 = jnp.float32
BF16 = jnp.bfloat16

LANES = 128
NORM_EPS = 1e-6
NEG_INF = -1e30
ROPE_THETA = 500000.0
ROT_FRACTION = 4
DEPTH = 4

MOBA_HEAD_DIM = 128
MOBA_BLOCK = 256
MOBA_TOPK = 3
MOBA_KEY_CHUNK = 1024
MOBA_QUERY_TILE = 512

MLA_NOPE_DIM = 128
MLA_ROPE_DIM = 64
MLA_V_DIM = 128
MLA_QK_PAD = 256

SWA_HEAD_DIM = 64
SWA_GROUP = 8
SWA_WINDOW = 128

DIFF_HEAD_DIM = 128

CONV_WIDTH = 3
CONV_HALO = 16

VT_CHUNK = 256
SUM_ROWS = 16
ATTN_HEADS_PER_STEP = 2
LOG2E = math.log2(math.e)
MXU_RESIDENT_BYTES = 8 * 1024 * 1024
VMEM_LIMIT = 56 * 1024 * 1024


def _cparams(*sem):
    return pltpu.CompilerParams(dimension_semantics=sem, vmem_limit_bytes=VMEM_LIMIT)


def _dot(a, b):
    return jnp.dot(a, b, preferred_element_type=F32)


def _dot_t(a, b):
    return lax.dot_general(a, b, (((1,), (1,)), ((), ())), preferred_element_type=F32)


def _rep(x, n):
    return x if n == 1 else jnp.concatenate([x] * n, axis=1)


def _rope_tables(seq, seg, rot):
    half = rot // 2
    inv = ROPE_THETA ** (-jnp.arange(half, dtype=F32) / half)
    ang = jnp.arange(seq, dtype=jnp.int32).astype(F32)[:, None] * inv[None, :]
    cos, sin = jnp.cos(ang), jnp.sin(ang)
    ones = jnp.ones((seq, seg - rot), F32)
    zeros_h = jnp.zeros((seq, half), F32)
    zeros_r = jnp.zeros((seq, seg - rot), F32)
    c = jnp.concatenate([cos, cos, ones], axis=1)
    s1 = jnp.concatenate([-sin, zeros_h, zeros_r], axis=1)
    s2 = jnp.concatenate([zeros_h, sin, zeros_r], axis=1)
    reps = LANES // seg
    return tuple(jnp.tile(t, (1, reps)) for t in (c, s1, s2))


def _rms_rows_to_bf16(x_ref, g_ref, xn_ref):
    x = x_ref[...]
    ms = jnp.mean(x * x, axis=-1, keepdims=True)
    xn_ref[...] = (x * lax.rsqrt(ms + NORM_EPS) * g_ref[...]).astype(BF16)


def _proj_kernel(*refs, pattern, rope_half, out_scale, n_chunks, has_rope, has_hg, kmean_rows, has_pair):
    it = iter(refs)
    x_ref, g_ref, w_ref = next(it), next(it), next(it)
    hg_ref = next(it) if has_hg else None
    if has_rope:
        c_ref, s1_ref, s2_ref = next(it), next(it), next(it)
    pair_ref = next(it) if has_pair else None
    o_ref = next(it)
    km_ref = next(it) if kmean_rows else None
    xn_ref = next(it)

    @pl.when(pl.program_id(1) == 0)
    def _():
        _rms_rows_to_bf16(x_ref, g_ref, xn_ref)

    acc = _dot(xn_ref[...], w_ref[...])
    for c in range(n_chunks):
        mode = pattern[c % len(pattern)]
        blk = acc[:, c * LANES:(c + 1) * LANES]
        if mode[0] == "norm":
            _, width, two, rope = mode
            sq = blk * blk
            if two:
                lo = lax.broadcasted_iota(jnp.int32, blk.shape, 1) < (LANES // 2)
                s_lo = jnp.sum(jnp.where(lo, sq, 0.0), axis=1, keepdims=True)
                s_hi = jnp.sum(jnp.where(lo, 0.0, sq), axis=1, keepdims=True)
                r = jnp.where(lo, lax.rsqrt(s_lo * (1.0 / width) + NORM_EPS),
                              lax.rsqrt(s_hi * (1.0 / width) + NORM_EPS))
            else:
                r = lax.rsqrt(jnp.sum(sq, axis=1, keepdims=True) * (1.0 / width) + NORM_EPS)
            blk = blk * r * hg_ref[:, c * LANES:(c + 1) * LANES]
            if rope:
                blk = (blk * c_ref[...]
                       + pltpu.roll(blk, LANES - rope_half, 1) * s1_ref[...]
                       + pltpu.roll(blk, rope_half, 1) * s2_ref[...])
            if out_scale != 1.0:
                blk = blk * out_scale
        if has_pair:
            o_ref[:, (2 * c) * LANES:(2 * c + 1) * LANES] = blk.astype(o_ref.dtype)
            o_ref[:, (2 * c + 1) * LANES:(2 * c + 2) * LANES] = pair_ref[...].astype(o_ref.dtype)
        else:
            o_ref[:, c * LANES:(c + 1) * LANES] = blk.astype(o_ref.dtype)
        if kmean_rows:
            for r_i in range(blk.shape[0] // kmean_rows):
                part = blk[r_i * kmean_rows:(r_i + 1) * kmean_rows, :]
                km_ref[r_i:r_i + 1, :, c * LANES:(c + 1) * LANES] = (
                    jnp.sum(part, axis=0, keepdims=True) * (1.0 / kmean_rows))[None]


def _norm_proj(x, gain, w, *, pattern, seq, name, xcol=0, hg=None, rope=None, rope_half=0, out_scale=1.0,
               out_dtype=None, kmean_rows=0, pair=None, tm=512):
    n_rows = x.shape[0]
    k_dim, n_out = w.shape
    assert k_dim * n_out * w.dtype.itemsize <= MXU_RESIDENT_BYTES
    tn = n_out
    out_dtype = out_dtype or BF16
    assert n_rows % tm == 0 and n_out % tn == 0 and seq % tm == 0
    n_chunks = tn // LANES
    assert n_chunks % len(pattern) == 0
    tiles_per_seq = seq // tm
    in_specs = [
        pl.BlockSpec((tm, k_dim), lambda i, j: (i, xcol)),
        pl.BlockSpec((1, k_dim), lambda i, j: (0, 0)),
        pl.BlockSpec((k_dim, tn), lambda i, j: (0, j)),
    ]
    args = [x, gain.reshape(1, k_dim).astype(F32), w]
    if hg is not None:
        in_specs.append(pl.BlockSpec((1, tn), lambda i, j: (0, j)))
        args.append(hg.reshape(1, n_out).astype(F32))
    if rope is not None:
        for t in rope:
            in_specs.append(pl.BlockSpec((tm, LANES), lambda i, j: (i % tiles_per_seq, 0)))
            args.append(t)
    out_mult = 1
    if pair is not None:
        pair_arr, pair_map = pair
        in_specs.append(pl.BlockSpec((tm, LANES), pair_map))
        args.append(pair_arr)
        out_mult = 2
    out_shape = [jax.ShapeDtypeStruct((n_rows, n_out * out_mult), out_dtype)]
    out_specs = [pl.BlockSpec((tm, tn * out_mult), lambda i, j: (i, j))]
    if kmean_rows:
        out_shape.append(jax.ShapeDtypeStruct((n_rows // kmean_rows, 1, n_out), F32))
        out_specs.append(pl.BlockSpec((tm // kmean_rows, 1, tn), lambda i, j: (i, 0, j)))
    kern = functools.partial(
        _proj_kernel, pattern=tuple(pattern), rope_half=rope_half, out_scale=out_scale, n_chunks=n_chunks,
        has_rope=rope is not None, has_hg=hg is not None, kmean_rows=kmean_rows, has_pair=pair is not None)
    res = pl.pallas_call(
        kern,
        grid=(n_rows // tm, n_out // tn),
        in_specs=in_specs,
        out_specs=out_specs,
        out_shape=out_shape,
        scratch_shapes=[pltpu.VMEM((tm, k_dim), BF16)],
        compiler_params=_cparams("parallel", "arbitrary"),
        name=name,
    )(*args)
    return res if kmean_rows else res[0]


def _proj_t_kernel(*refs, pattern, rope_half, out_scale, n_chunks, has_rope, has_hg, pos_chunk):
    it = iter(refs)
    x_ref, g_ref, w_ref = next(it), next(it), next(it)
    hg_ref = next(it) if has_hg else None
    if has_rope:
        c_ref, s1_ref, s2_ref = next(it), next(it), next(it)
    o_ref = next(it)
    xn_ref = next(it)

    @pl.when(pl.program_id(1) == 0)
    def _():
        _rms_rows_to_bf16(x_ref, g_ref, xn_ref)

    acc = _dot_t(w_ref[...], xn_ref[...])
    tm = acc.shape[1]
    for c in range(n_chunks):
        mode = pattern[c % len(pattern)]
        blk = acc[c * LANES:(c + 1) * LANES, :]
        if mode[0] == "norm":
            _, width, rope = mode
            r = lax.rsqrt(jnp.sum(blk * blk, axis=0, keepdims=True) * (1.0 / width) + NORM_EPS)
            blk = blk * r * _rep(hg_ref[c * LANES:(c + 1) * LANES, :], tm // LANES)
            if rope:
                up = jnp.concatenate([blk[rope_half:, :], blk[:rope_half, :]], axis=0)
                dn = jnp.concatenate([blk[-rope_half:, :], blk[:-rope_half, :]], axis=0)
                blk = blk * c_ref[...] + up * s1_ref[...] + dn * s2_ref[...]
            if out_scale != 1.0:
                blk = blk * out_scale
        for pc in range(tm // pos_chunk):
            o_ref[pc, c * LANES:(c + 1) * LANES, :] = blk[:, pc * pos_chunk:(pc + 1) * pos_chunk].astype(o_ref.dtype)


def _norm_proj_t(x, gain, w_t, *, pattern, seq, name, pos_chunk, xcol=0, hg=None, rope=None, rope_half=0,
                 out_scale=1.0, tm=512):
    n_rows = x.shape[0]
    n_out, k_dim = w_t.shape
    assert k_dim * n_out * w_t.dtype.itemsize <= MXU_RESIDENT_BYTES
    tn = n_out
    assert n_rows % tm == 0 and n_out % tn == 0 and seq % tm == 0 and tm % pos_chunk == 0
    n_chunks = tn // LANES
    assert n_chunks % len(pattern) == 0
    tiles_per_seq = seq // tm
    in_specs = [
        pl.BlockSpec((tm, k_dim), lambda i, j: (i, xcol)),
        pl.BlockSpec((1, k_dim), lambda i, j: (0, 0)),
        pl.BlockSpec((tn, k_dim), lambda i, j: (j, 0)),
    ]
    args = [x, gain.reshape(1, k_dim).astype(F32), w_t]
    if hg is not None:
        in_specs.append(pl.BlockSpec((tn, LANES), lambda i, j: (j, 0)))
        args.append(jnp.broadcast_to(hg.astype(F32)[:, None], (n_out, LANES)))
    if rope is not None:
        for t in rope:
            in_specs.append(pl.BlockSpec((LANES, tm), lambda i, j: (0, i % tiles_per_seq)))
            args.append(t.T)
    kern = functools.partial(
        _proj_t_kernel, pattern=tuple(pattern), rope_half=rope_half, out_scale=out_scale, n_chunks=n_chunks,
        has_rope=rope is not None, has_hg=hg is not None, pos_chunk=pos_chunk)
    per_tile = tm // pos_chunk
    return pl.pallas_call(
        kern,
        grid=(n_rows // tm, n_out // tn),
        in_specs=in_specs,
        out_specs=pl.BlockSpec((per_tile, tn, pos_chunk), lambda i, j: (i, j, 0)),
        out_shape=jax.ShapeDtypeStruct((n_rows // pos_chunk, n_out, pos_chunk), BF16),
        scratch_shapes=[pltpu.VMEM((tm, k_dim), BF16)],
        compiler_params=_cparams("parallel", "arbitrary"),
        name=name,
    )(*args)


def _mm_res_kernel(a_ref, w_ref, r_ref, o_ref):
    o_ref[...] = r_ref[...] + _dot(a_ref[...], w_ref[...])


def _matmul_residual(a, w, res, *, name):
    n_rows, k_dim = a.shape
    n_out = w.shape[1]
    if k_dim * n_out * w.dtype.itemsize <= MXU_RESIDENT_BYTES:
        tm, tn = 512, n_out
    else:
        tm, tn = 1024, 512
    tm = min(tm, n_rows)
    assert n_rows % tm == 0 and n_out % tn == 0
    return pl.pallas_call(
        _mm_res_kernel,
        grid=(n_rows // tm, n_out // tn),
        in_specs=[
            pl.BlockSpec((tm, k_dim), lambda i, j: (i, 0)),
            pl.BlockSpec((k_dim, tn), lambda i, j: (0, j)),
            pl.BlockSpec((tm, tn), lambda i, j: (i, j)),
        ],
        out_specs=pl.BlockSpec((tm, tn), lambda i, j: (i, j)),
        out_shape=jax.ShapeDtypeStruct((n_rows, n_out), F32),
        compiler_params=_cparams("parallel", "parallel"),
        name=name,
    )(a, w, res)


def _ffn_kernel(x_ref, xh_ref, g_ref, wg_ref, wu_ref, cw_ref, cb_ref, wd_ref, o_ref,
                xn_ref, gs_ref, us_ref, act_ref, acc_ref, *, tm, tiles_per_seq, n_tiles, n_split):
    i = pl.program_id(0)
    j = pl.program_id(1)
    h = CONV_HALO

    @pl.when(j == 0)
    def _():
        _rms_rows_to_bf16(xh_ref, g_ref, xn_ref.at[0:h, :])
        _rms_rows_to_bf16(x_ref, g_ref, xn_ref.at[h:h + tm, :])
        acc_ref[...] = x_ref[...]
        act_ref[1] = jnp.zeros(act_ref.shape[1:], act_ref.dtype)

    seq_start = (i % tiles_per_seq) == 0
    tf = wg_ref.shape[1]
    w = tf // n_split

    def step(parity):
        for s in range(n_split):
            cols = slice(s * w, (s + 1) * w)
            g = _dot(xn_ref[...], wg_ref[:, cols])
            gs_ref[0:h, cols] = jnp.where(seq_start, 0.0, g[0:h, :])
            gs_ref[h:h + tm, cols] = g[h:h + tm, :]
            us_ref[:, cols] = _dot(xn_ref[h:h + tm, :], wu_ref[:, cols])
        acc_ref[...] += _dot(act_ref[1 - parity], wd_ref[...])
        for s in range(n_split):
            cols = slice(s * w, (s + 1) * w)
            y = (cw_ref[0:1, cols] * gs_ref[h - 2:h - 2 + tm, cols]
                 + cw_ref[1:2, cols] * gs_ref[h - 1:h - 1 + tm, cols]
                 + cw_ref[2:3, cols] * gs_ref[h:h + tm, cols] + cb_ref[:, cols])
            act = y * (1.0 / (1.0 + jnp.exp(-y))) * us_ref[:, cols]
            act_ref[parity, :, cols] = act.astype(act_ref.dtype)

    for parity in range(2):
        @pl.when((j < n_tiles) & (j % 2 == parity))
        def _():
            step(parity)

    @pl.when(j == n_tiles)
    def _():
        o_ref[...] = acc_ref[...] + _dot(act_ref[(n_tiles - 1) % 2], wd_ref[...])


def _ffn(x, gain, wg, wu, conv_w, conv_b, wd, *, seq, tm=512, tf=512, n_split=2):
    n_rows, d = x.shape
    f = wg.shape[1]
    assert n_rows % tm == 0 and f % tf == 0 and seq % tm == 0 and tm % CONV_HALO == 0
    assert (tf // n_split) % LANES == 0
    tiles_per_seq = seq // tm
    halo_blocks = tm // CONV_HALO
    n_tiles = f // tf
    kern = functools.partial(_ffn_kernel, tm=tm, tiles_per_seq=tiles_per_seq, n_tiles=n_tiles, n_split=n_split)
    up_tile = lambda i, j: (0, jnp.minimum(j, n_tiles - 1))
    return pl.pallas_call(
        kern,
        grid=(n_rows // tm, n_tiles + 1),
        in_specs=[
            pl.BlockSpec((tm, d), lambda i, j: (i, 0)),
            pl.BlockSpec((CONV_HALO, d), lambda i, j: (jnp.maximum(i * halo_blocks - 1, 0), 0)),
            pl.BlockSpec((1, d), lambda i, j: (0, 0)),
            pl.BlockSpec((d, tf), up_tile),
            pl.BlockSpec((d, tf), up_tile),
            pl.BlockSpec((CONV_WIDTH, tf), up_tile),
            pl.BlockSpec((1, tf), up_tile),
            pl.BlockSpec((tf, d), lambda i, j: (jnp.maximum(j - 1, 0), 0)),
        ],
        out_specs=pl.BlockSpec((tm, d), lambda i, j: (i, 0)),
        out_shape=jax.ShapeDtypeStruct((n_rows, d), F32),
        scratch_shapes=[pltpu.VMEM((CONV_HALO + tm, d), BF16), pltpu.VMEM((CONV_HALO + tm, tf), F32),
                        pltpu.VMEM((tm, tf), F32), pltpu.VMEM((2, tm, tf), BF16), pltpu.VMEM((tm, d), F32)],
        compiler_params=_cparams("parallel", "arbitrary"),
        name="ffn",
    )(x, x, gain.reshape(1, d).astype(F32), wg, wu, conv_w.astype(F32), conv_b.reshape(1, f).astype(F32), wd)


def _flash_t_kernel(*refs, tq, ck, hps, n_comp, comp_dk, moba_topk, diff_lambda_init):
    it = iter(refs)
    q_ref, k_ref, v_ref = next(it), next(it), next(it)
    km_ref = next(it) if moba_topk else None
    if diff_lambda_init is not None:
        lq1_ref, lk1_ref, lq2_ref, lk2_ref, gs_ref = (next(it) for _ in range(5))
    o_ref = next(it)
    s_ref, p_ref, a_ref, m_ref, acc_ref = (next(it) for _ in range(5))
    dv = acc_ref.shape[1] - SUM_ROWS
    dq = q_ref.shape[1] // hps
    dk = k_ref.shape[1] // hps
    ns = hps * n_comp

    i = pl.program_id(2)
    m_ref[...] = jnp.full(m_ref.shape, NEG_INF, F32)
    acc_ref[...] = jnp.zeros(acc_ref.shape, F32)

    q_parts = []
    for h in range(hps):
        qt = q_ref[0, h * dq:(h + 1) * dq, :]
        if moba_topk:
            nb = km_ref.shape[0]
            km = km_ref[:, h * dq:(h + 1) * dq]
            km_hi = km.astype(BF16)
            km_lo = (km - km_hi.astype(F32)).astype(BF16)
            gate = _dot(km_hi, qt) + _dot(km_lo, qt)
            row = lax.broadcasted_iota(jnp.int32, gate.shape, 0)
            own = i * (tq // MOBA_BLOCK) + lax.broadcasted_iota(jnp.int32, gate.shape, 1) // MOBA_BLOCK
            neg = jnp.float32(-jnp.inf)
            gate = jnp.where(row < own, gate, neg)
            allowed = row == own
            for _ in range(moba_topk):
                mx = jnp.max(gate, axis=0, keepdims=True)
                first = jnp.min(jnp.where(gate == mx, row, nb), axis=0, keepdims=True)
                pick = (row == first) & (mx > neg)
                allowed = allowed | pick
                gate = jnp.where(pick, neg, gate)
            bias = jnp.where(allowed, 0.0, NEG_INF)
            bias = jnp.concatenate([bias, jnp.zeros((LANES - nb, tq), F32)], axis=0).astype(BF16)
            q_parts.append(jnp.concatenate([qt, bias], axis=0))
        else:
            q_parts += [qt[c * comp_dk:(c + 1) * comp_dk, :] for c in range(n_comp)]

    slabs = ck // VT_CHUNK

    def qk(chunk, slot):
        start = pl.multiple_of(chunk * ck, ck)
        for h in range(hps):
            for c in range(n_comp):
                st = h * n_comp + c
                col = h * dk + c * comp_dk
                s_ref[slot * ns + st] = _dot(k_ref[pl.ds(start, ck), col:col + comp_dk], q_parts[st])

    def sm(chunk, slot, masked):
        for st in range(ns):
            s = s_ref[slot * ns + st]
            if masked:
                key = chunk * ck + lax.broadcasted_iota(jnp.int32, s.shape, 0)
                qry = i * tq + lax.broadcasted_iota(jnp.int32, s.shape, 1)
                s = jnp.where(key <= qry, s, NEG_INF)
            m_prev = m_ref[st]
            m_next = jnp.maximum(m_prev, jnp.max(s, axis=0, keepdims=True))
            m_ref[st] = m_next
            a_ref[slot * ns + st] = jnp.exp2(m_prev - m_next)
            p_ref[slot * ns + st] = jnp.exp2(s - m_next).astype(BF16)

    ones = jnp.ones((SUM_ROWS, ck), BF16)

    def pv(chunk, slot):
        for h in range(hps):
            vt = [v_ref[chunk * slabs + t, h * dv:(h + 1) * dv, :] for t in range(slabs)]
            vt = vt[0] if slabs == 1 else jnp.concatenate(vt, axis=1)
            vt = jnp.concatenate([vt, ones], axis=0)
            for c in range(n_comp):
                st = h * n_comp + c
                acc_ref[st] = acc_ref[st] * a_ref[slot * ns + st] + _dot(vt, p_ref[slot * ns + st])

    n = (i * tq) // ck + 1
    qk(0, 0)

    @pl.when(n >= 2)
    def _():
        sm(0, 0, False)
        qk(1, 1)

    def pair_body(u, carry):
        c0 = 2 * u
        qk(c0 + 2, 0)
        pv(c0, 0)
        sm(c0 + 1, 1, False)
        qk(c0 + 3, 1)
        pv(c0 + 1, 1)
        sm(c0 + 2, 0, False)
        return carry

    pairs = jnp.maximum(n - 2, 0) // 2
    lax.fori_loop(0, pairs, pair_body, 0)

    @pl.when((n % 2 == 1) & (n >= 3))
    def _():
        qk(n - 1, 0)
        pv(n - 3, 0)
        sm(n - 2, 1, False)
        pv(n - 2, 1)

    @pl.when(n % 2 == 1)
    def _():
        sm(n - 1, 0, True)
        pv(n - 1, 0)

    @pl.when(n % 2 == 0)
    def _():
        pv(n - 2, 0)
        sm(n - 1, 1, True)
        pv(n - 1, 1)

    for h in range(hps):
        outs = [acc_ref[h * n_comp + c, 0:dv, :] / acc_ref[h * n_comp + c, dv:dv + 1, :]
                for c in range(n_comp)]
        if diff_lambda_init is None:
            o = outs[0]
        else:
            lam = (jnp.exp(jnp.sum(lq1_ref[...] * lk1_ref[...], axis=1, keepdims=True))
                   - jnp.exp(jnp.sum(lq2_ref[...] * lk2_ref[...], axis=1, keepdims=True))
                   + diff_lambda_init)
            o = outs[0] - lam * outs[1]
            ms = jnp.mean(o * o, axis=0, keepdims=True)
            o = o * lax.rsqrt(ms + NORM_EPS) * _rep(gs_ref[...], tq // LANES) * (1.0 - diff_lambda_init)
        o_ref[:, h * dv:(h + 1) * dv] = o.T.astype(o_ref.dtype)


def _flash_t(q_t, k, v_t, *, batch, seq, heads, tq, ck, dk, dv, name, hps=1, n_comp=1, kmean=None, diff=None):
    assert seq % ck == 0 and ck % tq == 0 and tq % VT_CHUNK == 0 and heads % hps == 0
    nq = seq // tq
    dq = q_t.shape[1] // heads
    assert q_t.shape == (batch * nq, heads * dq, tq) and k.shape == (batch * seq, heads * dk)
    assert v_t.shape == (batch * seq // VT_CHUNK, heads * dv, VT_CHUNK)
    slabs_per_seq = seq // VT_CHUNK
    in_specs = [
        pl.BlockSpec((1, hps * dq, tq), lambda b, h, i: (b * nq + i, h, 0)),
        pl.BlockSpec((seq, hps * dk), lambda b, h, i: (b, h)),
        pl.BlockSpec((slabs_per_seq, hps * dv, VT_CHUNK), lambda b, h, i: (b, h, 0)),
    ]
    args = [q_t, k, v_t]
    moba_topk = 0
    if kmean is not None:
        nb = seq // MOBA_BLOCK
        assert tq % MOBA_BLOCK == 0 and nb <= LANES and dq + LANES == dk
        moba_topk = min(MOBA_TOPK, nb)
        in_specs.append(pl.BlockSpec((nb, hps * dq), lambda b, h, i: (b, h)))
        args.append(kmean)
    lambda_init = None
    if diff is not None:
        lq1, lk1, lq2, lk2, g_sub, lambda_init = diff
        for a in (lq1, lk1, lq2, lk2):
            in_specs.append(pl.BlockSpec((1, a.shape[0]), lambda b, h, i: (0, 0)))
            args.append(a.reshape(1, -1).astype(F32))
        in_specs.append(pl.BlockSpec((dv, LANES), lambda b, h, i: (0, 0)))
        args.append(jnp.broadcast_to(g_sub.astype(F32)[:, None], (dv, LANES)))
    kern = functools.partial(_flash_t_kernel, tq=tq, ck=ck, hps=hps, n_comp=n_comp, comp_dk=dk // n_comp,
                             moba_topk=moba_topk, diff_lambda_init=lambda_init)
    ns = hps * n_comp
    scratch = [
        pltpu.VMEM((2 * ns, ck, tq), F32),
        pltpu.VMEM((2 * ns, ck, tq), BF16),
        pltpu.VMEM((2 * ns, 1, tq), F32),
        pltpu.VMEM((ns, 1, tq), F32),
        pltpu.VMEM((ns, dv + SUM_ROWS, tq), F32),
    ]
    return pl.pallas_call(
        kern,
        grid=(batch, heads // hps, nq),
        in_specs=in_specs,
        out_specs=pl.BlockSpec((tq, hps * dv), lambda b, h, i: (b * nq + i, h)),
        out_shape=jax.ShapeDtypeStruct((batch * seq, heads * dv), BF16),
        scratch_shapes=scratch,
        compiler_params=_cparams("parallel", "parallel", "arbitrary"),
        name=name,
    )(*args)


def _swa_kernel(sink_ref, q_ref, k_ref, v_ref, o_ref, *, tq, window, group):
    kh = pl.program_id(1)
    i = pl.program_id(2)
    half = LANES // 2
    n_sub = tq // window
    pairs = group // 2

    def sub(sb, carry):
        r0 = i * tq + sb * window
        k0 = jnp.maximum(r0 - window, 0)
        qrow = pl.multiple_of(sb * window, window)
        kstart = pl.multiple_of(k0, window)
        kk = k_ref[pl.ds(kstart, 2 * window), :]
        vv = v_ref[pl.ds(kstart, 2 * window), :]
        qpos = r0 + lax.broadcasted_iota(jnp.int32, (window, 2 * window), 0)
        kpos = k0 + lax.broadcasted_iota(jnp.int32, (window, 2 * window), 1)
        valid = (kpos <= qpos) & (kpos > qpos - window)
        lane_lo = lax.broadcasted_iota(jnp.int32, (window, LANES), 1) < half
        for pr in range(pairs):
            q2 = q_ref[pl.ds(qrow, window), pr * LANES:(pr + 1) * LANES]
            outs = []
            for e in range(2):
                keep = lane_lo if e == 0 else jnp.logical_not(lane_lo)
                qe = jnp.where(keep, q2, jnp.zeros_like(q2))
                s = jnp.where(valid, _dot_t(qe, kk), NEG_INF)
                sink = sink_ref[kh * group + 2 * pr + e]
                m = jnp.maximum(jnp.max(s, axis=1, keepdims=True), sink)
                p = jnp.exp(s - m)
                denom = jnp.sum(p, axis=1, keepdims=True) + jnp.exp(sink - m)
                p = p / denom
                outs.append(_dot(p.astype(BF16), vv))
            o_pair = jnp.where(lane_lo, outs[0], outs[1])
            o_ref[pl.ds(qrow, window), pr * LANES:(pr + 1) * LANES] = o_pair.astype(o_ref.dtype)
        return carry

    lax.fori_loop(0, n_sub, sub, 0)


def _swa_attention(q, k, v, sinks, *, batch, seq, kv_heads, tq=512):
    tq = min(tq, seq)
    assert seq % tq == 0 and tq % SWA_WINDOW == 0 and seq >= 2 * SWA_WINDOW
    nq = seq // tq
    gw = SWA_GROUP * SWA_HEAD_DIM
    kern = functools.partial(_swa_kernel, tq=tq, window=SWA_WINDOW, group=SWA_GROUP)
    return pl.pallas_call(
        kern,
        grid=(batch, kv_heads, nq),
        in_specs=[
            pl.BlockSpec(memory_space=pltpu.SMEM),
            pl.BlockSpec((tq, gw), lambda b, h, i: (b * nq + i, h)),
            pl.BlockSpec((seq, LANES), lambda b, h, i: (b, h)),
            pl.BlockSpec((seq, LANES), lambda b, h, i: (b, h)),
        ],
        out_specs=pl.BlockSpec((tq, gw), lambda b, h, i: (b * nq + i, h)),
        out_shape=jax.ShapeDtypeStruct((batch * seq, kv_heads * gw), BF16),
        compiler_params=_cparams("parallel", "parallel", "arbitrary"),
        name="swa_attn",
    )(sinks.astype(F32), q, k, v)


def _tile_gain(g, n):
    return jnp.tile(g.astype(F32), n)


def _moba_layer(x, an, wq, wk, wv, gq, gk, wo, *, batch, seq):
    dh, blk = MOBA_HEAD_DIM, MOBA_BLOCK
    heads = wq.shape[1] // dh
    rot = dh // ROT_FRACTION
    rope = _rope_tables(seq, LANES, rot)
    tiles_per_seq = seq // 512
    tq = min(MOBA_QUERY_TILE, seq)
    q_t = _norm_proj_t(x, an, wq.T.astype(BF16), pattern=[("norm", dh, True)], seq=seq, pos_chunk=tq,
                       hg=_tile_gain(gq, heads), rope=rope, rope_half=rot // 2, out_scale=dh ** -0.5 * LOG2E,
                       name="moba_q_proj")
    block_id = jnp.arange(seq, dtype=jnp.int32)[:, None] // blk
    onehot = (block_id == jnp.arange(LANES, dtype=jnp.int32)[None, :]).astype(F32)
    k, kmean = _norm_proj(x, an, wk.astype(BF16), pattern=[("norm", dh, False, True)], seq=seq,
                          hg=_tile_gain(gk, heads), rope=rope, rope_half=rot // 2, kmean_rows=blk,
                          pair=(onehot, lambda i, j: (i % tiles_per_seq, 0)), name="moba_k_proj")
    v_t = _norm_proj_t(x, an, wv.T.astype(BF16), pattern=[("copy",)], seq=seq, pos_chunk=VT_CHUNK,
                       name="moba_v_proj")
    kmean = kmean.reshape(kmean.shape[0], kmean.shape[2])
    o = _flash_t(q_t, k, v_t, batch=batch, seq=seq, heads=heads, tq=tq, ck=min(MOBA_KEY_CHUNK, seq), dk=2 * dh,
                 dv=dh, hps=ATTN_HEADS_PER_STEP, kmean=kmean, name="moba_attn")
    return _matmul_residual(o, wo.astype(BF16), x, name="moba_out_proj")


def _mla_layer(x, an, wq_a, g_qa, wq_b, wkv_a, g_kva, wkv_b, g_qn, g_kn, g_qr, g_kr, wo, *, batch, seq):
    d = x.shape[1]
    nope, rd, vd, pad = MLA_NOPE_DIM, MLA_ROPE_DIM, MLA_V_DIM, MLA_QK_PAD
    q_rank = wq_a.shape[1]
    kv_rank = wkv_a.shape[1] - rd
    heads = wq_b.shape[1] // (nope + rd)
    rope = _rope_tables(seq, LANES, rd)
    zpad = lambda n: jnp.zeros((n,), F32)
    tq = min(512, seq)

    w_a = jnp.concatenate([wq_a, wkv_a, jnp.zeros((d, LANES - rd), F32)], axis=1).astype(BF16)
    n_a = w_a.shape[1]
    chunks_a = n_a // LANES
    hg_a = jnp.concatenate([jnp.ones((q_rank + kv_rank,), F32), g_kr.astype(F32), zpad(LANES - rd)])
    pat_a = [("copy",)] * (chunks_a - 1) + [("norm", rd, False, True)]
    lat = _norm_proj(x, an, w_a, pattern=pat_a, seq=seq, hg=hg_a, rope=rope, rope_half=rd // 2,
                     out_dtype=F32, name="mla_latent_proj")

    wq_b3 = wq_b.reshape(q_rank, heads, nope + rd)
    wq_p = jnp.concatenate([wq_b3, jnp.zeros((q_rank, heads, pad - nope - rd), F32)], axis=2)
    wq_p = wq_p.reshape(q_rank, heads * pad)
    hg_q = jnp.tile(jnp.concatenate([g_qn.astype(F32), g_qr.astype(F32), zpad(pad - nope - rd)]), heads)
    pat_q = [("norm", nope, False), ("norm", rd, True)]
    q_t = _norm_proj_t(lat, g_qa, wq_p.T.astype(BF16), pattern=pat_q, seq=seq, pos_chunk=tq, xcol=0, hg=hg_q,
                       rope=rope, rope_half=rd // 2, out_scale=(nope + rd) ** -0.5 * LOG2E, name="mla_q_proj")

    wkv3 = wkv_b.reshape(kv_rank, heads, nope + vd)
    w_k = wkv3[:, :, :nope].reshape(kv_rank, heads * nope).astype(BF16)
    w_v = wkv3[:, :, nope:].reshape(kv_rank, heads * vd)
    assert q_rank == kv_rank
    kr_col = (q_rank + kv_rank) // LANES
    k = _norm_proj(lat, g_kva, w_k, pattern=[("norm", nope, False, False)], seq=seq, xcol=1,
                   hg=_tile_gain(g_kn, heads), pair=(lat, lambda i, j: (i, kr_col)), name="mla_k_proj")
    v_t = _norm_proj_t(lat, g_kva, w_v.T.astype(BF16), pattern=[("copy",)], seq=seq, pos_chunk=VT_CHUNK, xcol=1,
                       name="mla_v_proj")
    o = _flash_t(q_t, k, v_t, batch=batch, seq=seq, heads=heads, tq=tq, ck=tq, dk=pad, dv=vd,
                 hps=ATTN_HEADS_PER_STEP, name="mla_attn")
    return _matmul_residual(o, wo.astype(BF16), x, name="mla_out_proj")


def _swa_layer(x, an, wq, wk, wv, gq, gk, sinks, wo, *, batch, seq):
    d = x.shape[1]
    dh = SWA_HEAD_DIM
    hq = wq.shape[1] // dh
    hkv = wk.shape[1] // dh
    rot = dh // ROT_FRACTION
    rope = _rope_tables(seq, dh, rot)

    def dup(w):
        w3 = w.reshape(d, hkv, dh)
        return jnp.concatenate([w3, w3], axis=2).reshape(d, hkv * 2 * dh).astype(BF16)

    q = _norm_proj(x, an, wq.astype(BF16), pattern=[("norm", dh, True, True)], seq=seq, hg=_tile_gain(gq, hq),
                   rope=rope, rope_half=rot // 2, out_scale=dh ** -0.5, name="swa_q_proj")
    k = _norm_proj(x, an, dup(wk), pattern=[("norm", 2 * dh, False, True)], seq=seq,
                   hg=_tile_gain(gk, 2 * hkv), rope=rope, rope_half=rot // 2, name="swa_k_proj")
    v = _norm_proj(x, an, dup(wv), pattern=[("copy",)], seq=seq, name="swa_v_proj")
    o = _swa_attention(q, k, v, sinks, batch=batch, seq=seq, kv_heads=hkv)
    return _matmul_residual(o, wo.astype(BF16), x, name="swa_out_proj")


def _diff_layer(x, an, wq, wk, wv, gq, gk, lq1, lk1, lq2, lk2, g_sub, wo, lambda_init, *, batch, seq):
    dh = DIFF_HEAD_DIM
    heads = wq.shape[1] // (2 * dh)
    rot = dh // ROT_FRACTION
    rope = _rope_tables(seq, LANES, rot)
    tq = min(512, seq)
    q_t = _norm_proj_t(x, an, wq.T.astype(BF16), pattern=[("norm", dh, True)], seq=seq, pos_chunk=tq,
                       hg=_tile_gain(gq, 2 * heads), rope=rope, rope_half=rot // 2, out_scale=dh ** -0.5 * LOG2E,
                       name="diff_q_proj")
    k = _norm_proj(x, an, wk.astype(BF16), pattern=[("norm", dh, False, True)], seq=seq,
                   hg=_tile_gain(gk, 2 * heads), rope=rope, rope_half=rot // 2, name="diff_k_proj")
    v_t = _norm_proj_t(x, an, wv.T.astype(BF16), pattern=[("copy",)], seq=seq, pos_chunk=VT_CHUNK,
                       name="diff_v_proj")
    o = _flash_t(q_t, k, v_t, batch=batch, seq=seq, heads=heads, tq=tq, ck=tq, dk=2 * dh, dv=2 * dh, n_comp=2,
                 diff=(lq1, lk1, lq2, lk2, g_sub, lambda_init), name="diff_attn")
    return _matmul_residual(o, wo.astype(BF16), x, name="diff_out_proj")


def _ffn_layer(x, fn, w_gate, w_up, conv_w, conv_b, w_down, *, seq):
    return _ffn(x, fn, w_gate.astype(BF16), w_up.astype(BF16), conv_w, conv_b, w_down.astype(BF16), seq=seq)


def kernel(x, attn_norm, ffn_norm, moba_wq, moba_wk, moba_wv, moba_gq, moba_gk, moba_wo, mla_wq_a, mla_g_qa, mla_wq_b, mla_wkv_a, mla_g_kva, mla_wkv_b, mla_g_qn, mla_g_kn, mla_g_qr, mla_g_kr, mla_wo, swa_wq, swa_wk, swa_wv, swa_gq, swa_gk, swa_sinks, swa_wo, diff_wq, diff_wk, diff_wv, diff_gq, diff_gk, diff_lq1, diff_lk1, diff_lq2, diff_lk2, diff_g_sub, diff_wo, ffn_w_gate, ffn_w_up, ffn_conv_w, ffn_conv_b, ffn_w_down):
    batch, seq, d = x.shape
    h = x.reshape(batch * seq, d)
    kw = dict(batch=batch, seq=seq)
    for i in range(DEPTH):
        m, j = i % 4, i // 4
        if m == 0:
            h = _moba_layer(h, attn_norm[i], moba_wq[j], moba_wk[j], moba_wv[j], moba_gq[j], moba_gk[j],
                            moba_wo[j], **kw)
        elif m == 1:
            h = _mla_layer(h, attn_norm[i], mla_wq_a[j], mla_g_qa[j], mla_wq_b[j], mla_wkv_a[j], mla_g_kva[j],
                           mla_wkv_b[j], mla_g_qn[j], mla_g_kn[j], mla_g_qr[j], mla_g_kr[j], mla_wo[j], **kw)
        elif m == 2:
            h = _swa_layer(h, attn_norm[i], swa_wq[j], swa_wk[j], swa_wv[j], swa_gq[j], swa_gk[j],
                           swa_sinks[j], swa_wo[j], **kw)
        else:
            lambda_init = 0.8 - 0.6 * math.exp(-0.3 * i)
            h = _diff_layer(h, attn_norm[i], diff_wq[j], diff_wk[j], diff_wv[j], diff_gq[j], diff_gk[j],
                            diff_lq1[j], diff_lk1[j], diff_lq2[j], diff_lk2[j], diff_g_sub[j], diff_wo[j],
                            lambda_init, **kw)
        h = _ffn_layer(h, ffn_norm[i], ffn_w_gate[i], ffn_w_up[i], ffn_conv_w[i], ffn_conv_b[i],
                       ffn_w_down[i], seq=seq)
    return h.reshape(batch, seq, d)
```

```python
import functools
import math

import jax
import jax.numpy as jnp
from jax import lax
from jax.experimental import pallas as pl
from jax.experimental.pallas import tpu as pltpu

F32 = jnp.float32
BF16 = jnp.bfloat16

LANES = 128
NORM_EPS = 1e-6
NEG_INF = -1e30
ROPE_THETA = 500000.0
ROT_FRACTION = 4
DEPTH = 4

MOBA_HEAD_DIM = 128
MOBA_BLOCK = 256
MOBA_TOPK = 3
MOBA_KEY_CHUNK = 1024
MOBA_QUERY_TILE = 512

MLA_NOPE_DIM = 128
MLA_ROPE_DIM = 64
MLA_V_DIM = 128
MLA_QK_PAD = 256

SWA_HEAD_DIM = 64
SWA_GROUP = 8
SWA_WINDOW = 128

DIFF_HEAD_DIM = 128

CONV_WIDTH = 3
CONV_HALO = 16

VT_CHUNK = 256
SUM_ROWS = 16
ATTN_HEADS_PER_STEP = 2
LOG2E = math.log2(math.e)
MXU_RESIDENT_BYTES = 8 * 1024 * 1024
PROJ_GROUP = 2
VMEM_LIMIT = 56 * 1024 * 1024


def _cparams(*sem):
    return pltpu.CompilerParams(dimension_semantics=sem, vmem_limit_bytes=VMEM_LIMIT)


def _dot(a, b):
    return jnp.dot(a, b, preferred_element_type=F32)


def _dot_t(a, b):
    return lax.dot_general(a, b, (((1,), (1,)), ((), ())), preferred_element_type=F32)


def _rep(x, n):
    return x if n == 1 else jnp.concatenate([x] * n, axis=1)


def _rope_tables(seq, seg, rot):
    half = rot // 2
    inv = ROPE_THETA ** (-jnp.arange(half, dtype=F32) / half)
    ang = jnp.arange(seq, dtype=jnp.int32).astype(F32)[:, None] * inv[None, :]
    cos, sin = jnp.cos(ang), jnp.sin(ang)
    ones = jnp.ones((seq, seg - rot), F32)
    zeros_h = jnp.zeros((seq, half), F32)
    zeros_r = jnp.zeros((seq, seg - rot), F32)
    c = jnp.concatenate([cos, cos, ones], axis=1)
    s1 = jnp.concatenate([-sin, zeros_h, zeros_r], axis=1)
    s2 = jnp.concatenate([zeros_h, sin, zeros_r], axis=1)
    reps = LANES // seg
    return tuple(jnp.tile(t, (1, reps)) for t in (c, s1, s2))


def _rms_rows_to_bf16(x_ref, g_ref, xn_ref):
    x = x_ref[...]
    ms = jnp.mean(x * x, axis=-1, keepdims=True)
    xn_ref[...] = (x * lax.rsqrt(ms + NORM_EPS) * g_ref[...]).astype(BF16)


def _proj_kernel(*refs, pattern, rope_half, out_scale, n_chunks, has_rope, has_hg, kmean_rows, has_pair):
    it = iter(refs)
    x_ref, g_ref, w_ref = next(it), next(it), next(it)
    hg_ref = next(it) if has_hg else None
    if has_rope:
        c_ref, s1_ref, s2_ref = next(it), next(it), next(it)
    pair_ref = next(it) if has_pair else None
    o_ref = next(it)
    km_ref = next(it) if kmean_rows else None
    xn_ref = next(it)

    @pl.when(pl.program_id(1) == 0)
    def _():
        _rms_rows_to_bf16(x_ref, g_ref, xn_ref)

    for c in range(n_chunks):
        if c % PROJ_GROUP == 0:
            hi = min(c + PROJ_GROUP, n_chunks)
            acc = _dot(xn_ref[...], w_ref[:, c * LANES:hi * LANES])
        mode = pattern[c % len(pattern)]
        blk = acc[:, (c % PROJ_GROUP) * LANES:(c % PROJ_GROUP + 1) * LANES]
        if mode[0] == "norm":
            _, width, two, rope = mode
            sq = blk * blk
            if two:
                lo = lax.broadcasted_iota(jnp.int32, blk.shape, 1) < (LANES // 2)
                s_lo = jnp.sum(jnp.where(lo, sq, 0.0), axis=1, keepdims=True)
                s_hi = jnp.sum(jnp.where(lo, 0.0, sq), axis=1, keepdims=True)
                r = jnp.where(lo, lax.rsqrt(s_lo * (1.0 / width) + NORM_EPS),
                              lax.rsqrt(s_hi * (1.0 / width) + NORM_EPS))
            else:
                r = lax.rsqrt(jnp.sum(sq, axis=1, keepdims=True) * (1.0 / width) + NORM_EPS)
            blk = blk * r * hg_ref[:, c * LANES:(c + 1) * LANES]
            if rope:
                blk = (blk * c_ref[...]
                       + pltpu.roll(blk, LANES - rope_half, 1) * s1_ref[...]
                       + pltpu.roll(blk, rope_half, 1) * s2_ref[...])
            if out_scale != 1.0:
                blk = blk * out_scale
        if has_pair:
            o_ref[:, (2 * c) * LANES:(2 * c + 1) * LANES] = blk.astype(o_ref.dtype)
            o_ref[:, (2 * c + 1) * LANES:(2 * c + 2) * LANES] = pair_ref[...].astype(o_ref.dtype)
        else:
            o_ref[:, c * LANES:(c + 1) * LANES] = blk.astype(o_ref.dtype)
        if kmean_rows:
            for r_i in range(blk.shape[0] // kmean_rows):
                part = blk[r_i * kmean_rows:(r_i + 1) * kmean_rows, :]
                km_ref[r_i:r_i + 1, :, c * LANES:(c + 1) * LANES] = (
                    jnp.sum(part, axis=0, keepdims=True) * (1.0 / kmean_rows))[None]


def _norm_proj(x, gain, w, *, pattern, seq, name, xcol=0, hg=None, rope=None, rope_half=0, out_scale=1.0,
               out_dtype=None, kmean_rows=0, pair=None, tm=512):
    n_rows = x.shape[0]
    k_dim, n_out = w.shape
    assert k_dim * n_out * w.dtype.itemsize <= MXU_RESIDENT_BYTES
    tn = n_out
    out_dtype = out_dtype or BF16
    assert n_rows % tm == 0 and n_out % tn == 0 and seq % tm == 0
    n_chunks = tn // LANES
    assert n_chunks % len(pattern) == 0
    tiles_per_seq = seq // tm
    in_specs = [
        pl.BlockSpec((tm, k_dim), lambda i, j: (i, xcol)),
        pl.BlockSpec((1, k_dim), lambda i, j: (0, 0)),
        pl.BlockSpec((k_dim, tn), lambda i, j: (0, j)),
    ]
    args = [x, gain.reshape(1, k_dim).astype(F32), w]
    if hg is not None:
        in_specs.append(pl.BlockSpec((1, tn), lambda i, j: (0, j)))
        args.append(hg.reshape(1, n_out).astype(F32))
    if rope is not None:
        for t in rope:
            in_specs.append(pl.BlockSpec((tm, LANES), lambda i, j: (i % tiles_per_seq, 0)))
            args.append(t)
    out_mult = 1
    if pair is not None:
        pair_arr, pair_map = pair
        in_specs.append(pl.BlockSpec((tm, LANES), pair_map))
        args.append(pair_arr)
        out_mult = 2
    out_shape = [jax.ShapeDtypeStruct((n_rows, n_out * out_mult), out_dtype)]
    out_specs = [pl.BlockSpec((tm, tn * out_mult), lambda i, j: (i, j))]
    if kmean_rows:
        out_shape.append(jax.ShapeDtypeStruct((n_rows // kmean_rows, 1, n_out), F32))
        out_specs.append(pl.BlockSpec((tm // kmean_rows, 1, tn), lambda i, j: (i, 0, j)))
    kern = functools.partial(
        _proj_kernel, pattern=tuple(pattern), rope_half=rope_half, out_scale=out_scale, n_chunks=n_chunks,
        has_rope=rope is not None, has_hg=hg is not None, kmean_rows=kmean_rows, has_pair=pair is not None)
    res = pl.pallas_call(
        kern,
        grid=(n_rows // tm, n_out // tn),
        in_specs=in_specs,
        out_specs=out_specs,
        out_shape=out_shape,
        scratch_shapes=[pltpu.VMEM((tm, k_dim), BF16)],
        compiler_params=_cparams("parallel", "arbitrary"),
        name=name,
    )(*args)
    return res if kmean_rows else res[0]


def _proj_t_kernel(*refs, pattern, rope_half, out_scale, n_chunks, has_rope, has_hg, pos_chunk):
    it = iter(refs)
    x_ref, g_ref, w_ref = next(it), next(it), next(it)
    hg_ref = next(it) if has_hg else None
    if has_rope:
        c_ref, s1_ref, s2_ref = next(it), next(it), next(it)
    o_ref = next(it)
    xn_ref = next(it)

    @pl.when(pl.program_id(1) == 0)
    def _():
        _rms_rows_to_bf16(x_ref, g_ref, xn_ref)

    tm = xn_ref.shape[0]
    for c in range(n_chunks):
        if c % PROJ_GROUP == 0:
            hi = min(c + PROJ_GROUP, n_chunks)
            acc = _dot_t(w_ref[c * LANES:hi * LANES, :], xn_ref[...])
        mode = pattern[c % len(pattern)]
        blk = acc[(c % PROJ_GROUP) * LANES:(c % PROJ_GROUP + 1) * LANES, :]
        if mode[0] == "norm":
            _, width, rope = mode
            r = lax.rsqrt(jnp.sum(blk * blk, axis=0, keepdims=True) * (1.0 / width) + NORM_EPS)
            blk = blk * r * _rep(hg_ref[c * LANES:(c + 1) * LANES, :], tm // LANES)
            if rope:
                up = jnp.concatenate([blk[rope_half:, :], blk[:rope_half, :]], axis=0)
                dn = jnp.concatenate([blk[-rope_half:, :], blk[:-rope_half, :]], axis=0)
                blk = blk * c_ref[...] + up * s1_ref[...] + dn * s2_ref[...]
            if out_scale != 1.0:
                blk = blk * out_scale
        for pc in range(tm // pos_chunk):
            o_ref[pc, c * LANES:(c + 1) * LANES, :] = blk[:, pc * pos_chunk:(pc + 1) * pos_chunk].astype(o_ref.dtype)


def _norm_proj_t(x, gain, w_t, *, pattern, seq, name, pos_chunk, xcol=0, hg=None, rope=None, rope_half=0,
                 out_scale=1.0, tm=512):
    n_rows = x.shape[0]
    n_out, k_dim = w_t.shape
    assert k_dim * n_out * w_t.dtype.itemsize <= MXU_RESIDENT_BYTES
    tn = n_out
    assert n_rows % tm == 0 and n_out % tn == 0 and seq % tm == 0 and tm % pos_chunk == 0
    n_chunks = tn // LANES
    assert n_chunks % len(pattern) == 0
    tiles_per_seq = seq // tm
    in_specs = [
        pl.BlockSpec((tm, k_dim), lambda i, j: (i, xcol)),
        pl.BlockSpec((1, k_dim), lambda i, j: (0, 0)),
        pl.BlockSpec((tn, k_dim), lambda i, j: (j, 0)),
    ]
    args = [x, gain.reshape(1, k_dim).astype(F32), w_t]
    if hg is not None:
        in_specs.append(pl.BlockSpec((tn, LANES), lambda i, j: (j, 0)))
        args.append(jnp.broadcast_to(hg.astype(F32)[:, None], (n_out, LANES)))
    if rope is not None:
        for t in rope:
            in_specs.append(pl.BlockSpec((LANES, tm), lambda i, j: (0, i % tiles_per_seq)))
            args.append(t.T)
    kern = functools.partial(
        _proj_t_kernel, pattern=tuple(pattern), rope_half=rope_half, out_scale=out_scale, n_chunks=n_chunks,
        has_rope=rope is not None, has_hg=hg is not None, pos_chunk=pos_chunk)
    per_tile = tm // pos_chunk
    return pl.pallas_call(
        kern,
        grid=(n_rows // tm, n_out // tn),
        in_specs=in_specs,
        out_specs=pl.BlockSpec((per_tile, tn, pos_chunk), lambda i, j: (i, j, 0)),
        out_shape=jax.ShapeDtypeStruct((n_rows // pos_chunk, n_out, pos_chunk), BF16),
        scratch_shapes=[pltpu.VMEM((tm, k_dim), BF16)],
        compiler_params=_cparams("parallel", "arbitrary"),
        name=name,
    )(*args)


def _mm_res_kernel(a_ref, w_ref, r_ref, o_ref):
    o_ref[...] = r_ref[...] + _dot(a_ref[...], w_ref[...])


def _matmul_residual(a, w, res, *, name):
    n_rows, k_dim = a.shape
    n_out = w.shape[1]
    if k_dim * n_out * w.dtype.itemsize <= MXU_RESIDENT_BYTES:
        tm, tn = 512, n_out
    else:
        tm, tn = 1024, 512
    tm = min(tm, n_rows)
    assert n_rows % tm == 0 and n_out % tn == 0
    return pl.pallas_call(
        _mm_res_kernel,
        grid=(n_rows // tm, n_out // tn),
        in_specs=[
            pl.BlockSpec((tm, k_dim), lambda i, j: (i, 0)),
            pl.BlockSpec((k_dim, tn), lambda i, j: (0, j)),
            pl.BlockSpec((tm, tn), lambda i, j: (i, j)),
        ],
        out_specs=pl.BlockSpec((tm, tn), lambda i, j: (i, j)),
        out_shape=jax.ShapeDtypeStruct((n_rows, n_out), F32),
        compiler_params=_cparams("parallel", "parallel"),
        name=name,
    )(a, w, res)


def _ffn_up_kernel(x_ref, xh_ref, g_ref, wg_ref, wu_ref, cw_ref, cb_ref, o_ref, xn_ref, gs_ref, us_ref,
                   *, tm, tiles_per_seq, n_split):
    i = pl.program_id(0)
    h = CONV_HALO

    @pl.when(pl.program_id(1) == 0)
    def _():
        _rms_rows_to_bf16(xh_ref, g_ref, xn_ref.at[0:h, :])
        _rms_rows_to_bf16(x_ref, g_ref, xn_ref.at[h:h + tm, :])

    seq_start = (i % tiles_per_seq) == 0
    tf = o_ref.shape[1]
    w = tf // n_split
    for s in range(n_split):
        cols = slice(s * w, (s + 1) * w)
        g = _dot(xn_ref[...], wg_ref[:, cols])
        gs_ref[0:h, cols] = jnp.where(seq_start, 0.0, g[0:h, :])
        gs_ref[h:h + tm, cols] = g[h:h + tm, :]
        us_ref[:, cols] = _dot(xn_ref[h:h + tm, :], wu_ref[:, cols])
    for s in range(n_split):
        cols = slice(s * w, (s + 1) * w)
        y = (cw_ref[0:1, cols] * gs_ref[h - 2:h - 2 + tm, cols]
             + cw_ref[1:2, cols] * gs_ref[h - 1:h - 1 + tm, cols]
             + cw_ref[2:3, cols] * gs_ref[h:h + tm, cols] + cb_ref[:, cols])
        act = y * (1.0 / (1.0 + jnp.exp(-y))) * us_ref[:, cols]
        o_ref[:, cols] = act.astype(o_ref.dtype)


def _ffn_up(x, gain, wg, wu, conv_w, conv_b, *, seq, tm=1024, tf=512, n_split=2):
    n_rows, d = x.shape
    f = wg.shape[1]
    tm = min(tm, seq)
    assert n_rows % tm == 0 and f % tf == 0 and seq % tm == 0 and tm % CONV_HALO == 0
    assert (tf // n_split) % LANES == 0
    tiles_per_seq = seq // tm
    halo_blocks = tm // CONV_HALO
    kern = functools.partial(_ffn_up_kernel, tm=tm, tiles_per_seq=tiles_per_seq, n_split=n_split)
    return pl.pallas_call(
        kern,
        grid=(n_rows // tm, f // tf),
        in_specs=[
            pl.BlockSpec((tm, d), lambda i, j: (i, 0)),
            pl.BlockSpec((CONV_HALO, d), lambda i, j: (jnp.maximum(i * halo_blocks - 1, 0), 0)),
            pl.BlockSpec((1, d), lambda i, j: (0, 0)),
            pl.BlockSpec((d, tf), lambda i, j: (0, j)),
            pl.BlockSpec((d, tf), lambda i, j: (0, j)),
            pl.BlockSpec((CONV_WIDTH, tf), lambda i, j: (0, j)),
            pl.BlockSpec((1, tf), lambda i, j: (0, j)),
        ],
        out_specs=pl.BlockSpec((tm, tf), lambda i, j: (i, j)),
        out_shape=jax.ShapeDtypeStruct((n_rows, f), BF16),
        scratch_shapes=[pltpu.VMEM((CONV_HALO + tm, d), BF16), pltpu.VMEM((CONV_HALO + tm, tf), F32),
                        pltpu.VMEM((tm, tf), F32)],
        compiler_params=_cparams("parallel", "arbitrary"),
        name="ffn_up",
    )(x, x, gain.reshape(1, d).astype(F32), wg, wu, conv_w.astype(F32), conv_b.reshape(1, f).astype(F32))


def _flash_t_kernel(*refs, tq, ck, hps, n_comp, comp_dk, moba_topk, diff_lambda_init):
    it = iter(refs)
    q_ref, k_ref, v_ref = next(it), next(it), next(it)
    km_ref = next(it) if moba_topk else None
    if diff_lambda_init is not None:
        lq1_ref, lk1_ref, lq2_ref, lk2_ref, gs_ref = (next(it) for _ in range(5))
    o_ref = next(it)
    s_ref, p_ref, a_ref, m_ref, acc_ref = (next(it) for _ in range(5))
    dv = acc_ref.shape[1] - SUM_ROWS
    dq = q_ref.shape[1] // hps
    dk = k_ref.shape[1] // hps
    ns = hps * n_comp

    i = pl.program_id(2)
    m_ref[...] = jnp.full(m_ref.shape, NEG_INF, F32)
    acc_ref[...] = jnp.zeros(acc_ref.shape, F32)

    q_parts = []
    for h in range(hps):
        qt = q_ref[0, h * dq:(h + 1) * dq, :]
        if moba_topk:
            nb = km_ref.shape[0]
            km = km_ref[:, h * dq:(h + 1) * dq]
            km_hi = km.astype(BF16)
            km_lo = (km - km_hi.astype(F32)).astype(BF16)
            gate = _dot(km_hi, qt) + _dot(km_lo, qt)
            row = lax.broadcasted_iota(jnp.int32, gate.shape, 0)
            own = i * (tq // MOBA_BLOCK) + lax.broadcasted_iota(jnp.int32, gate.shape, 1) // MOBA_BLOCK
            neg = jnp.float32(-jnp.inf)
            gate = jnp.where(row < own, gate, neg)
            allowed = row == own
            for _ in range(moba_topk):
                mx = jnp.max(gate, axis=0, keepdims=True)
                first = jnp.min(jnp.where(gate == mx, row, nb), axis=0, keepdims=True)
                pick = (row == first) & (mx > neg)
                allowed = allowed | pick
                gate = jnp.where(pick, neg, gate)
            bias = jnp.where(allowed, 0.0, NEG_INF)
            bias = jnp.concatenate([bias, jnp.zeros((LANES - nb, tq), F32)], axis=0).astype(BF16)
            q_parts.append(jnp.concatenate([qt, bias], axis=0))
        else:
            q_parts += [qt[c * comp_dk:(c + 1) * comp_dk, :] for c in range(n_comp)]

    slabs = ck // VT_CHUNK

    def qk(chunk, slot):
        start = pl.multiple_of(chunk * ck, ck)
        for h in range(hps):
            for c in range(n_comp):
                st = h * n_comp + c
                col = h * dk + c * comp_dk
                s_ref[slot * ns + st] = _dot(k_ref[pl.ds(start, ck), col:col + comp_dk], q_parts[st])

    def sm(chunk, slot, masked):
        for st in range(ns):
            s = s_ref[slot * ns + st]
            if masked:
                key = chunk * ck + lax.broadcasted_iota(jnp.int32, s.shape, 0)
                qry = i * tq + lax.broadcasted_iota(jnp.int32, s.shape, 1)
                s = jnp.where(key <= qry, s, NEG_INF)
            m_prev = m_ref[st]
            m_next = jnp.maximum(m_prev, jnp.max(s, axis=0, keepdims=True))
            m_ref[st] = m_next
            a_ref[slot * ns + st] = jnp.exp2(m_prev - m_next)
            p_ref[slot * ns + st] = jnp.exp2(s - m_next).astype(BF16)

    ones = jnp.ones((SUM_ROWS, ck), BF16)

    def pv(chunk, slot):
        for h in range(hps):
            vt = [v_ref[chunk * slabs + t, h * dv:(h + 1) * dv, :] for t in range(slabs)]
            vt = vt[0] if slabs == 1 else jnp.concatenate(vt, axis=1)
            vt = jnp.concatenate([vt, ones], axis=0)
            for c in range(n_comp):
                st = h * n_comp + c
                acc_ref[st] = acc_ref[st] * a_ref[slot * ns + st] + _dot(vt, p_ref[slot * ns + st])

    n = (i * tq) // ck + 1
    qk(0, 0)

    @pl.when(n >= 2)
    def _():
        sm(0, 0, False)
        qk(1, 1)

    def pair_body(u, carry):
        c0 = 2 * u
        qk(c0 + 2, 0)
        pv(c0, 0)
        sm(c0 + 1, 1, False)
        qk(c0 + 3, 1)
        pv(c0 + 1, 1)
        sm(c0 + 2, 0, False)
        return carry

    pairs = jnp.maximum(n - 2, 0) // 2
    lax.fori_loop(0, pairs, pair_body, 0)

    @pl.when((n % 2 == 1) & (n >= 3))
    def _():
        qk(n - 1, 0)
        pv(n - 3, 0)
        sm(n - 2, 1, False)
        pv(n - 2, 1)

    @pl.when(n % 2 == 1)
    def _():
        sm(n - 1, 0, True)
        pv(n - 1, 0)

    @pl.when(n % 2 == 0)
    def _():
        pv(n - 2, 0)
        sm(n - 1, 1, True)
        pv(n - 1, 1)

    for h in range(hps):
        outs = [acc_ref[h * n_comp + c, 0:dv, :] / acc_ref[h * n_comp + c, dv:dv + 1, :]
                for c in range(n_comp)]
        if diff_lambda_init is None:
            o = outs[0]
        else:
            lam = (jnp.exp(jnp.sum(lq1_ref[...] * lk1_ref[...], axis=1, keepdims=True))
                   - jnp.exp(jnp.sum(lq2_ref[...] * lk2_ref[...], axis=1, keepdims=True))
                   + diff_lambda_init)
            o = outs[0] - lam * outs[1]
            ms = jnp.mean(o * o, axis=0, keepdims=True)
            o = o * lax.rsqrt(ms + NORM_EPS) * _rep(gs_ref[...], tq // LANES) * (1.0 - diff_lambda_init)
        o_ref[:, h * dv:(h + 1) * dv] = o.T.astype(o_ref.dtype)


def _flash_t(q_t, k, v_t, *, batch, seq, heads, tq, ck, dk, dv, name, hps=1, n_comp=1, kmean=None, diff=None):
    assert seq % ck == 0 and ck % tq == 0 and tq % VT_CHUNK == 0 and heads % hps == 0
    nq = seq // tq
    dq = q_t.shape[1] // heads
    assert q_t.shape == (batch * nq, heads * dq, tq) and k.shape == (batch * seq, heads * dk)
    assert v_t.shape == (batch * seq // VT_CHUNK, heads * dv, VT_CHUNK)
    slabs_per_seq = seq // VT_CHUNK
    in_specs = [
        pl.BlockSpec((1, hps * dq, tq), lambda b, h, i: (b * nq + i, h, 0)),
        pl.BlockSpec((seq, hps * dk), lambda b, h, i: (b, h)),
        pl.BlockSpec((slabs_per_seq, hps * dv, VT_CHUNK), lambda b, h, i: (b, h, 0)),
    ]
    args = [q_t, k, v_t]
    moba_topk = 0
    if kmean is not None:
        nb = seq // MOBA_BLOCK
        assert tq % MOBA_BLOCK == 0 and nb <= LANES and dq + LANES == dk
        moba_topk = min(MOBA_TOPK, nb)
        in_specs.append(pl.BlockSpec((nb, hps * dq), lambda b, h, i: (b, h)))
        args.append(kmean)
    lambda_init = None
    if diff is not None:
        lq1, lk1, lq2, lk2, g_sub, lambda_init = diff
        for a in (lq1, lk1, lq2, lk2):
            in_specs.append(pl.BlockSpec((1, a.shape[0]), lambda b, h, i: (0, 0)))
            args.append(a.reshape(1, -1).astype(F32))
        in_specs.append(pl.BlockSpec((dv, LANES), lambda b, h, i: (0, 0)))
        args.append(jnp.broadcast_to(g_sub.astype(F32)[:, None], (dv, LANES)))
    kern = functools.partial(_flash_t_kernel, tq=tq, ck=ck, hps=hps, n_comp=n_comp, comp_dk=dk // n_comp,
                             moba_topk=moba_topk, diff_lambda_init=lambda_init)
    ns = hps * n_comp
    scratch = [
        pltpu.VMEM((2 * ns, ck, tq), F32),
        pltpu.VMEM((2 * ns, ck, tq), BF16),
        pltpu.VMEM((2 * ns, 1, tq), F32),
        pltpu.VMEM((ns, 1, tq), F32),
        pltpu.VMEM((ns, dv + SUM_ROWS, tq), F32),
    ]
    return pl.pallas_call(
        kern,
        grid=(batch, heads // hps, nq),
        in_specs=in_specs,
        out_specs=pl.BlockSpec((tq, hps * dv), lambda b, h, i: (b * nq + i, h)),
        out_shape=jax.ShapeDtypeStruct((batch * seq, heads * dv), BF16),
        scratch_shapes=scratch,
        compiler_params=_cparams("parallel", "parallel", "arbitrary"),
        name=name,
    )(*args)


def _swa_kernel(sink_ref, q_ref, k_ref, v_ref, o_ref, *, tq, window, group):
    kh = pl.program_id(1)
    i = pl.program_id(2)
    half = LANES // 2
    n_sub = tq // window
    pairs = group // 2

    def sub(sb, carry):
        r0 = i * tq + sb * window
        k0 = jnp.maximum(r0 - window, 0)
        qrow = sb * window
        kstart = pl.multiple_of(k0, window)
        kk = k_ref[pl.ds(kstart, 2 * window), :]
        vv = v_ref[pl.ds(kstart, 2 * window), :]
        qpos = r0 + lax.broadcasted_iota(jnp.int32, (window, 2 * window), 0)
        kpos = k0 + lax.broadcasted_iota(jnp.int32, (window, 2 * window), 1)
        valid = (kpos <= qpos) & (kpos > qpos - window)
        lane_lo = lax.broadcasted_iota(jnp.int32, (window, LANES), 1) < half
        for pr in range(pairs):
            q2 = q_ref[pl.ds(qrow, window), pr * LANES:(pr + 1) * LANES]
            outs = []
            for e in range(2):
                keep = lane_lo if e == 0 else jnp.logical_not(lane_lo)
                qe = jnp.where(keep, q2, jnp.zeros_like(q2))
                s = jnp.where(valid, _dot_t(qe, kk), NEG_INF)
                sink = sink_ref[kh * group + 2 * pr + e]
                m = jnp.maximum(jnp.max(s, axis=1, keepdims=True), sink)
                p = jnp.exp(s - m)
                denom = jnp.sum(p, axis=1, keepdims=True) + jnp.exp(sink - m)
                p = p / denom
                outs.append(_dot(p.astype(BF16), vv))
            o_pair = jnp.where(lane_lo, outs[0], outs[1])
            o_ref[pl.ds(qrow, window), pr * LANES:(pr + 1) * LANES] = o_pair.astype(o_ref.dtype)
        return carry

    for sb in range(n_sub):
        sub(sb, 0)


def _swa_attention(q, k, v, sinks, *, batch, seq, kv_heads, tq=512):
    tq = min(tq, seq)
    assert seq % tq == 0 and tq % SWA_WINDOW == 0 and seq >= 2 * SWA_WINDOW
    nq = seq // tq
    gw = SWA_GROUP * SWA_HEAD_DIM
    kern = functools.partial(_swa_kernel, tq=tq, window=SWA_WINDOW, group=SWA_GROUP)
    return pl.pallas_call(
        kern,
        grid=(batch, kv_heads, nq),
        in_specs=[
            pl.BlockSpec(memory_space=pltpu.SMEM),
            pl.BlockSpec((tq, gw), lambda b, h, i: (b * nq + i, h)),
            pl.BlockSpec((seq, LANES), lambda b, h, i: (b, h)),
            pl.BlockSpec((seq, LANES), lambda b, h, i: (b, h)),
        ],
        out_specs=pl.BlockSpec((tq, gw), lambda b, h, i: (b * nq + i, h)),
        out_shape=jax.ShapeDtypeStruct((batch * seq, kv_heads * gw), BF16),
        compiler_params=_cparams("parallel", "parallel", "arbitrary"),
        name="swa_attn",
    )(sinks.astype(F32), q, k, v)


def _tile_gain(g, n):
    return jnp.tile(g.astype(F32), n)


def _moba_layer(x, an, wq, wk, wv, gq, gk, wo, *, batch, seq):
    dh, blk = MOBA_HEAD_DIM, MOBA_BLOCK
    heads = wq.shape[1] // dh
    rot = dh // ROT_FRACTION
    rope = _rope_tables(seq, LANES, rot)
    tiles_per_seq = seq // 512
    tq = min(MOBA_QUERY_TILE, seq)
    q_t = _norm_proj_t(x, an, wq.T.astype(BF16), pattern=[("norm", dh, True)], seq=seq, pos_chunk=tq,
                       hg=_tile_gain(gq, heads), rope=rope, rope_half=rot // 2, out_scale=dh ** -0.5 * LOG2E,
                       name="moba_q_proj")
    block_id = jnp.arange(seq, dtype=jnp.int32)[:, None] // blk
    onehot = (block_id == jnp.arange(LANES, dtype=jnp.int32)[None, :]).astype(F32)
    k, kmean = _norm_proj(x, an, wk.astype(BF16), pattern=[("norm", dh, False, True)], seq=seq,
                          hg=_tile_gain(gk, heads), rope=rope, rope_half=rot // 2, kmean_rows=blk,
                          pair=(onehot, lambda i, j: (i % tiles_per_seq, 0)), name="moba_k_proj")
    v_t = _norm_proj_t(x, an, wv.T.astype(BF16), pattern=[("copy",)], seq=seq, pos_chunk=VT_CHUNK,
                       name="moba_v_proj")
    kmean = kmean.reshape(kmean.shape[0], kmean.shape[2])
    o = _flash_t(q_t, k, v_t, batch=batch, seq=seq, heads=heads, tq=tq, ck=min(MOBA_KEY_CHUNK, seq), dk=2 * dh,
                 dv=dh, hps=ATTN_HEADS_PER_STEP, kmean=kmean, name="moba_attn")
    return _matmul_residual(o, wo.astype(BF16), x, name="moba_out_proj")


def _mla_layer(x, an, wq_a, g_qa, wq_b, wkv_a, g_kva, wkv_b, g_qn, g_kn, g_qr, g_kr, wo, *, batch, seq):
    d = x.shape[1]
    nope, rd, vd, pad = MLA_NOPE_DIM, MLA_ROPE_DIM, MLA_V_DIM, MLA_QK_PAD
    q_rank = wq_a.shape[1]
    kv_rank = wkv_a.shape[1] - rd
    heads = wq_b.shape[1] // (nope + rd)
    rope = _rope_tables(seq, LANES, rd)
    zpad = lambda n: jnp.zeros((n,), F32)
    tq = min(512, seq)

    w_a = jnp.concatenate([wq_a, wkv_a, jnp.zeros((d, LANES - rd), F32)], axis=1).astype(BF16)
    n_a = w_a.shape[1]
    chunks_a = n_a // LANES
    hg_a = jnp.concatenate([jnp.ones((q_rank + kv_rank,), F32), g_kr.astype(F32), zpad(LANES - rd)])
    pat_a = [("copy",)] * (chunks_a - 1) + [("norm", rd, False, True)]
    lat = _norm_proj(x, an, w_a, pattern=pat_a, seq=seq, hg=hg_a, rope=rope, rope_half=rd // 2,
                     out_dtype=F32, name="mla_latent_proj")

    wq_b3 = wq_b.reshape(q_rank, heads, nope + rd)
    wq_p = jnp.concatenate([wq_b3, jnp.zeros((q_rank, heads, pad - nope - rd), F32)], axis=2)
    wq_p = wq_p.reshape(q_rank, heads * pad)
    hg_q = jnp.tile(jnp.concatenate([g_qn.astype(F32), g_qr.astype(F32), zpad(pad - nope - rd)]), heads)
    pat_q = [("norm", nope, False), ("norm", rd, True)]
    q_t = _norm_proj_t(lat, g_qa, wq_p.T.astype(BF16), pattern=pat_q, seq=seq, pos_chunk=tq, xcol=0, hg=hg_q,
                       rope=rope, rope_half=rd // 2, out_scale=(nope + rd) ** -0.5 * LOG2E, name="mla_q_proj")

    wkv3 = wkv_b.reshape(kv_rank, heads, nope + vd)
    w_k = wkv3[:, :, :nope].reshape(kv_rank, heads * nope).astype(BF16)
    w_v = wkv3[:, :, nope:].reshape(kv_rank, heads * vd)
    assert q_rank == kv_rank
    kr_col = (q_rank + kv_rank) // LANES
    k = _norm_proj(lat, g_kva, w_k, pattern=[("norm", nope, False, False)], seq=seq, xcol=1,
                   hg=_tile_gain(g_kn, heads), pair=(lat, lambda i, j: (i, kr_col)), name="mla_k_proj")
    v_t = _norm_proj_t(lat, g_kva, w_v.T.astype(BF16), pattern=[("copy",)], seq=seq, pos_chunk=VT_CHUNK, xcol=1,
                       name="mla_v_proj")
    o = _flash_t(q_t, k, v_t, batch=batch, seq=seq, heads=heads, tq=tq, ck=tq, dk=pad, dv=vd,
                 hps=ATTN_HEADS_PER_STEP, name="mla_attn")
    return _matmul_residual(o, wo.astype(BF16), x, name="mla_out_proj")


def _swa_layer(x, an, wq, wk, wv, gq, gk, sinks, wo, *, batch, seq):
    d = x.shape[1]
    dh = SWA_HEAD_DIM
    hq = wq.shape[1] // dh
    hkv = wk.shape[1] // dh
    rot = dh // ROT_FRACTION
    rope = _rope_tables(seq, dh, rot)

    def dup(w):
        w3 = w.reshape(d, hkv, dh)
        return jnp.concatenate([w3, w3], axis=2).reshape(d, hkv * 2 * dh).astype(BF16)

    q = _norm_proj(x, an, wq.astype(BF16), pattern=[("norm", dh, True, True)], seq=seq, hg=_tile_gain(gq, hq),
                   rope=rope, rope_half=rot // 2, out_scale=dh ** -0.5, name="swa_q_proj")
    k = _norm_proj(x, an, dup(wk), pattern=[("norm", 2 * dh, False, True)], seq=seq,
                   hg=_tile_gain(gk, 2 * hkv), rope=rope, rope_half=rot // 2, name="swa_k_proj")
    v = _norm_proj(x, an, dup(wv), pattern=[("copy",)], seq=seq, name="swa_v_proj")
    o = _swa_attention(q, k, v, sinks, batch=batch, seq=seq, kv_heads=hkv)
    return _matmul_residual(o, wo.astype(BF16), x, name="swa_out_proj")


def _diff_layer(x, an, wq, wk, wv, gq, gk, lq1, lk1, lq2, lk2, g_sub, wo, lambda_init, *, batch, seq):
    dh = DIFF_HEAD_DIM
    heads = wq.shape[1] // (2 * dh)
    rot = dh // ROT_FRACTION
    rope = _rope_tables(seq, LANES, rot)
    tq = min(512, seq)
    q_t = _norm_proj_t(x, an, wq.T.astype(BF16), pattern=[("norm", dh, True)], seq=seq, pos_chunk=tq,
                       hg=_tile_gain(gq, 2 * heads), rope=rope, rope_half=rot // 2, out_scale=dh ** -0.5 * LOG2E,
                       name="diff_q_proj")
    k = _norm_proj(x, an, wk.astype(BF16), pattern=[("norm", dh, False, True)], seq=seq,
                   hg=_tile_gain(gk, 2 * heads), rope=rope, rope_half=rot // 2, name="diff_k_proj")
    v_t = _norm_proj_t(x, an, wv.T.astype(BF16), pattern=[("copy",)], seq=seq, pos_chunk=VT_CHUNK,
                       name="diff_v_proj")
    o = _flash_t(q_t, k, v_t, batch=batch, seq=seq, heads=heads, tq=tq, ck=tq, dk=2 * dh, dv=2 * dh, n_comp=2,
                 diff=(lq1, lk1, lq2, lk2, g_sub, lambda_init), name="diff_attn")
    return _matmul_residual(o, wo.astype(BF16), x, name="diff_out_proj")


def _ffn_layer(x, fn, w_gate, w_up, conv_w, conv_b, w_down, *, seq):
    act = _ffn_up(x, fn, w_gate.astype(BF16), w_up.astype(BF16), conv_w, conv_b, seq=seq)
    return _matmul_residual(act, w_down.astype(BF16), x, name="ffn_down")


def kernel(x, attn_norm, ffn_norm, moba_wq, moba_wk, moba_wv, moba_gq, moba_gk, moba_wo, mla_wq_a, mla_g_qa, mla_wq_b, mla_wkv_a, mla_g_kva, mla_wkv_b, mla_g_qn, mla_g_kn, mla_g_qr, mla_g_kr, mla_wo, swa_wq, swa_wk, swa_wv, swa_gq, swa_gk, swa_sinks, swa_wo, diff_wq, diff_wk, diff_wv, diff_gq, diff_gk, diff_lq1, diff_lk1, diff_lq2, diff_lk2, diff_g_sub, diff_wo, ffn_w_gate, ffn_w_up, ffn_conv_w, ffn_conv_b, ffn_w_down):
    batch, seq, d = x.shape
    h = x.reshape(batch * seq, d)
    kw = dict(batch=batch, seq=seq)
    for i in range(DEPTH):
        m, j = i % 4, i // 4
        if m == 0:
            h = _moba_layer(h, attn_norm[i], moba_wq[j], moba_wk[j], moba_wv[j], moba_gq[j], moba_gk[j],
                            moba_wo[j], **kw)
        elif m == 1:
            h = _mla_layer(h, attn_norm[i], mla_wq_a[j], mla_g_qa[j], mla_wq_b[j], mla_wkv_a[j], mla_g_kva[j],
                           mla_wkv_b[j], mla_g_qn[j], mla_g_kn[j], mla_g_qr[j], mla_g_kr[j], mla_wo[j], **kw)
        elif m == 2:
            h = _swa_layer(h, attn_norm[i], swa_wq[j], swa_wk[j], swa_wv[j], swa_gq[j], swa_gk[j],
                           swa_sinks[j], swa_wo[j], **kw)
        else:
            lambda_init = 0.8 - 0.6 * math.exp(-0.3 * i)
            h = _diff_layer(h, attn_norm[i], diff_wq[j], diff_wk[j], diff_wv[j], diff_gq[j], diff_gk[j],
                            diff_lq1[j], diff_lk1[j], diff_lq2[j], diff_lk2[j], diff_g_sub[j], diff_wo[j],
                            lambda_init, **kw)
        h = _ffn_layer(h, ffn_norm[i], ffn_w_gate[i], ffn_w_up[i], ffn_conv_w[i], ffn_conv_b[i],
                       ffn_w_down[i], seq=seq)
    return h.reshape(batch, seq, d)
```

```python
import functools
import math

import jax
import jax.numpy as jnp
from jax import lax
from jax.experimental import pallas as pl
from jax.experimental.pallas import tpu as pltpu

F32 = jnp.float32
BF16 = jnp.bfloat16

LANES = 128
NORM_EPS = 1e-6
NEG_INF = -1e30
ROPE_THETA = 500000.0
ROT_FRACTION = 4
DEPTH = 4

MOBA_HEAD_DIM = 128
MOBA_BLOCK = 256
MOBA_TOPK = 3
MOBA_KEY_CHUNK = 512
MOBA_QUERY_TILE = 512

MLA_NOPE_DIM = 128
MLA_ROPE_DIM = 64
MLA_V_DIM = 128
MLA_QK_PAD = 256

SWA_HEAD_DIM = 64
SWA_GROUP = 8
SWA_WINDOW = 128

DIFF_HEAD_DIM = 128

CONV_WIDTH = 3
CONV_HALO = 16

VT_CHUNK = 256
SUM_ROWS = 16
ATTN_HEADS_PER_STEP = 2
LOG2E = math.log2(math.e)
MXU_RESIDENT_BYTES = 8 * 1024 * 1024
PROJ_GROUP = 2
VMEM_LIMIT = 56 * 1024 * 1024


def _cparams(*sem):
    return pltpu.CompilerParams(dimension_semantics=sem, vmem_limit_bytes=VMEM_LIMIT)


def _dot(a, b):
    return jnp.dot(a, b, preferred_element_type=F32)


def _dot_t(a, b):
    return lax.dot_general(a, b, (((1,), (1,)), ((), ())), preferred_element_type=F32)


def _rep(x, n):
    return x if n == 1 else jnp.concatenate([x] * n, axis=1)


def _rope_tables(seq, seg, rot):
    half = rot // 2
    inv = ROPE_THETA ** (-jnp.arange(half, dtype=F32) / half)
    ang = jnp.arange(seq, dtype=jnp.int32).astype(F32)[:, None] * inv[None, :]
    cos, sin = jnp.cos(ang), jnp.sin(ang)
    ones = jnp.ones((seq, seg - rot), F32)
    zeros_h = jnp.zeros((seq, half), F32)
    zeros_r = jnp.zeros((seq, seg - rot), F32)
    c = jnp.concatenate([cos, cos, ones], axis=1)
    s1 = jnp.concatenate([-sin, zeros_h, zeros_r], axis=1)
    s2 = jnp.concatenate([zeros_h, sin, zeros_r], axis=1)
    reps = LANES // seg
    return tuple(jnp.tile(t, (1, reps)) for t in (c, s1, s2))


def _rms_rows_to_bf16(x_ref, g_ref, xn_ref):
    x = x_ref[...]
    ms = jnp.mean(x * x, axis=-1, keepdims=True)
    xn_ref[...] = (x * lax.rsqrt(ms + NORM_EPS) * g_ref[...]).astype(BF16)


def _proj_kernel(*refs, pattern, rope_half, out_scale, n_chunks, has_rope, has_hg, kmean_rows, has_pair):
    it = iter(refs)
    x_ref, g_ref, w_ref = next(it), next(it), next(it)
    hg_ref = next(it) if has_hg else None
    if has_rope:
        c_ref, s1_ref, s2_ref = next(it), next(it), next(it)
    pair_ref = next(it) if has_pair else None
    o_ref = next(it)
    km_ref = next(it) if kmean_rows else None
    xn_ref, acc_ref = next(it), next(it)

    @pl.when(pl.program_id(1) == 0)
    def _():
        _rms_rows_to_bf16(x_ref, g_ref, xn_ref)

    for c in range(0, n_chunks, PROJ_GROUP):
        cols = slice(c * LANES, min(c + PROJ_GROUP, n_chunks) * LANES)
        acc_ref[:, cols] = _dot(xn_ref[...], w_ref[:, cols])
    for c in range(n_chunks):
        mode = pattern[c % len(pattern)]
        blk = acc_ref[:, c * LANES:(c + 1) * LANES]
        if mode[0] == "norm":
            _, width, two, rope = mode
            sq = blk * blk
            if two:
                lo = lax.broadcasted_iota(jnp.int32, blk.shape, 1) < (LANES // 2)
                s_lo = jnp.sum(jnp.where(lo, sq, 0.0), axis=1, keepdims=True)
                s_hi = jnp.sum(jnp.where(lo, 0.0, sq), axis=1, keepdims=True)
                r = jnp.where(lo, lax.rsqrt(s_lo * (1.0 / width) + NORM_EPS),
                              lax.rsqrt(s_hi * (1.0 / width) + NORM_EPS))
            else:
                r = lax.rsqrt(jnp.sum(sq, axis=1, keepdims=True) * (1.0 / width) + NORM_EPS)
            blk = blk * r * hg_ref[:, c * LANES:(c + 1) * LANES]
            if rope:
                blk = (blk * c_ref[...]
                       + pltpu.roll(blk, LANES - rope_half, 1) * s1_ref[...]
                       + pltpu.roll(blk, rope_half, 1) * s2_ref[...])
            if out_scale != 1.0:
                blk = blk * out_scale
        if has_pair:
            o_ref[:, (2 * c) * LANES:(2 * c + 1) * LANES] = blk.astype(o_ref.dtype)
            o_ref[:, (2 * c + 1) * LANES:(2 * c + 2) * LANES] = pair_ref[...].astype(o_ref.dtype)
        else:
            o_ref[:, c * LANES:(c + 1) * LANES] = blk.astype(o_ref.dtype)
        if kmean_rows:
            for r_i in range(blk.shape[0] // kmean_rows):
                part = blk[r_i * kmean_rows:(r_i + 1) * kmean_rows, :]
                km_ref[r_i:r_i + 1, :, c * LANES:(c + 1) * LANES] = (
                    jnp.sum(part, axis=0, keepdims=True) * (1.0 / kmean_rows))[None]


def _norm_proj(x, gain, w, *, pattern, seq, name, xcol=0, hg=None, rope=None, rope_half=0, out_scale=1.0,
               out_dtype=None, kmean_rows=0, pair=None, tm=512):
    n_rows = x.shape[0]
    k_dim, n_out = w.shape
    assert k_dim * n_out * w.dtype.itemsize <= MXU_RESIDENT_BYTES
    tn = n_out
    out_dtype = out_dtype or BF16
    assert n_rows % tm == 0 and n_out % tn == 0 and seq % tm == 0
    n_chunks = tn // LANES
    assert n_chunks % len(pattern) == 0
    tiles_per_seq = seq // tm
    in_specs = [
        pl.BlockSpec((tm, k_dim), lambda i, j: (i, xcol)),
        pl.BlockSpec((1, k_dim), lambda i, j: (0, 0)),
        pl.BlockSpec((k_dim, tn), lambda i, j: (0, j)),
    ]
    args = [x, gain.reshape(1, k_dim).astype(F32), w]
    if hg is not None:
        in_specs.append(pl.BlockSpec((1, tn), lambda i, j: (0, j)))
        args.append(hg.reshape(1, n_out).astype(F32))
    if rope is not None:
        for t in rope:
            in_specs.append(pl.BlockSpec((tm, LANES), lambda i, j: (i % tiles_per_seq, 0)))
            args.append(t)
    out_mult = 1
    if pair is not None:
        pair_arr, pair_map = pair
        in_specs.append(pl.BlockSpec((tm, LANES), pair_map))
        args.append(pair_arr)
        out_mult = 2
    out_shape = [jax.ShapeDtypeStruct((n_rows, n_out * out_mult), out_dtype)]
    out_specs = [pl.BlockSpec((tm, tn * out_mult), lambda i, j: (i, j))]
    if kmean_rows:
        out_shape.append(jax.ShapeDtypeStruct((n_rows // kmean_rows, 1, n_out), F32))
        out_specs.append(pl.BlockSpec((tm // kmean_rows, 1, tn), lambda i, j: (i, 0, j)))
    kern = functools.partial(
        _proj_kernel, pattern=tuple(pattern), rope_half=rope_half, out_scale=out_scale, n_chunks=n_chunks,
        has_rope=rope is not None, has_hg=hg is not None, kmean_rows=kmean_rows, has_pair=pair is not None)
    res = pl.pallas_call(
        kern,
        grid=(n_rows // tm, n_out // tn),
        in_specs=in_specs,
        out_specs=out_specs,
        out_shape=out_shape,
        scratch_shapes=[pltpu.VMEM((tm, k_dim), BF16), pltpu.VMEM((tm, tn), F32)],
        compiler_params=_cparams("parallel", "arbitrary"),
        name=name,
    )(*args)
    return res if kmean_rows else res[0]


def _proj_t_kernel(*refs, pattern, rope_half, out_scale, n_chunks, has_rope, has_hg, pos_chunk):
    it = iter(refs)
    x_ref, g_ref, w_ref = next(it), next(it), next(it)
    hg_ref = next(it) if has_hg else None
    if has_rope:
        c_ref, s1_ref, s2_ref = next(it), next(it), next(it)
    o_ref = next(it)
    xn_ref, acc_ref = next(it), next(it)

    @pl.when(pl.program_id(1) == 0)
    def _():
        _rms_rows_to_bf16(x_ref, g_ref, xn_ref)

    tm = xn_ref.shape[0]
    for c in range(0, n_chunks, PROJ_GROUP):
        rows = slice(c * LANES, min(c + PROJ_GROUP, n_chunks) * LANES)
        acc_ref[rows, :] = _dot_t(w_ref[rows, :], xn_ref[...])
    for c in range(n_chunks):
        mode = pattern[c % len(pattern)]
        blk = acc_ref[c * LANES:(c + 1) * LANES, :]
        if mode[0] == "norm":
            _, width, rope = mode
            r = lax.rsqrt(jnp.sum(blk * blk, axis=0, keepdims=True) * (1.0 / width) + NORM_EPS)
            blk = blk * r * _rep(hg_ref[c * LANES:(c + 1) * LANES, :], tm // LANES)
            if rope:
                up = jnp.concatenate([blk[rope_half:, :], blk[:rope_half, :]], axis=0)
                dn = jnp.concatenate([blk[-rope_half:, :], blk[:-rope_half, :]], axis=0)
                blk = blk * c_ref[...] + up * s1_ref[...] + dn * s2_ref[...]
            if out_scale != 1.0:
                blk = blk * out_scale
        for pc in range(tm // pos_chunk):
            o_ref[pc, c * LANES:(c + 1) * LANES, :] = blk[:, pc * pos_chunk:(pc + 1) * pos_chunk].astype(o_ref.dtype)


def _norm_proj_t(x, gain, w_t, *, pattern, seq, name, pos_chunk, xcol=0, hg=None, rope=None, rope_half=0,
                 out_scale=1.0, tm=512):
    n_rows = x.shape[0]
    n_out, k_dim = w_t.shape
    assert k_dim * n_out * w_t.dtype.itemsize <= MXU_RESIDENT_BYTES
    tn = n_out
    assert n_rows % tm == 0 and n_out % tn == 0 and seq % tm == 0 and tm % pos_chunk == 0
    n_chunks = tn // LANES
    assert n_chunks % len(pattern) == 0
    tiles_per_seq = seq // tm
    in_specs = [
        pl.BlockSpec((tm, k_dim), lambda i, j: (i, xcol)),
        pl.BlockSpec((1, k_dim), lambda i, j: (0, 0)),
        pl.BlockSpec((tn, k_dim), lambda i, j: (j, 0)),
    ]
    args = [x, gain.reshape(1, k_dim).astype(F32), w_t]
    if hg is not None:
        in_specs.append(pl.BlockSpec((tn, LANES), lambda i, j: (j, 0)))
        args.append(jnp.broadcast_to(hg.astype(F32)[:, None], (n_out, LANES)))
    if rope is not None:
        for t in rope:
            in_specs.append(pl.BlockSpec((LANES, tm), lambda i, j: (0, i % tiles_per_seq)))
            args.append(t.T)
    kern = functools.partial(
        _proj_t_kernel, pattern=tuple(pattern), rope_half=rope_half, out_scale=out_scale, n_chunks=n_chunks,
        has_rope=rope is not None, has_hg=hg is not None, pos_chunk=pos_chunk)
    per_tile = tm // pos_chunk
    return pl.pallas_call(
        kern,
        grid=(n_rows // tm, n_out // tn),
        in_specs=in_specs,
        out_specs=pl.BlockSpec((per_tile, tn, pos_chunk), lambda i, j: (i, j, 0)),
        out_shape=jax.ShapeDtypeStruct((n_rows // pos_chunk, n_out, pos_chunk), BF16),
        scratch_shapes=[pltpu.VMEM((tm, k_dim), BF16), pltpu.VMEM((tn, tm), F32)],
        compiler_params=_cparams("parallel", "arbitrary"),
        name=name,
    )(*args)


def _mm_res_kernel(a_ref, w_ref, r_ref, o_ref):
    o_ref[...] = r_ref[...] + _dot(a_ref[...], w_ref[...])


def _matmul_residual(a, w, res, *, name):
    n_rows, k_dim = a.shape
    n_out = w.shape[1]
    if k_dim * n_out * w.dtype.itemsize <= MXU_RESIDENT_BYTES:
        tm, tn = 512, n_out
    else:
        tm, tn = 1024, 512
    tm = min(tm, n_rows)
    assert n_rows % tm == 0 and n_out % tn == 0
    return pl.pallas_call(
        _mm_res_kernel,
        grid=(n_rows // tm, n_out // tn),
        in_specs=[
            pl.BlockSpec((tm, k_dim), lambda i, j: (i, 0)),
            pl.BlockSpec((k_dim, tn), lambda i, j: (0, j)),
            pl.BlockSpec((tm, tn), lambda i, j: (i, j)),
        ],
        out_specs=pl.BlockSpec((tm, tn), lambda i, j: (i, j)),
        out_shape=jax.ShapeDtypeStruct((n_rows, n_out), F32),
        compiler_params=_cparams("parallel", "parallel"),
        name=name,
    )(a, w, res)


def _ffn_up_kernel(x_ref, xh_ref, g_ref, wg_ref, wu_ref, cw_ref, cb_ref, o_ref, xn_ref, gs_ref, us_ref,
                   *, tm, tiles_per_seq, n_split):
    i = pl.program_id(0)
    h = CONV_HALO

    @pl.when(pl.program_id(1) == 0)
    def _():
        _rms_rows_to_bf16(xh_ref, g_ref, xn_ref.at[0:h, :])
        _rms_rows_to_bf16(x_ref, g_ref, xn_ref.at[h:h + tm, :])

    seq_start = (i % tiles_per_seq) == 0
    tf = o_ref.shape[1]
    w = tf // n_split
    for s in range(n_split):
        cols = slice(s * w, (s + 1) * w)
        g = _dot(xn_ref[...], wg_ref[:, cols])
        gs_ref[0:h, cols] = jnp.where(seq_start, 0.0, g[0:h, :])
        gs_ref[h:h + tm, cols] = g[h:h + tm, :]
    for s in range(n_split):
        cols = slice(s * w, (s + 1) * w)
        us_ref[:, cols] = _dot(xn_ref[h:h + tm, :], wu_ref[:, cols])
    for s in range(n_split):
        cols = slice(s * w, (s + 1) * w)
        y = (cw_ref[0:1, cols] * gs_ref[h - 2:h - 2 + tm, cols]
             + cw_ref[1:2, cols] * gs_ref[h - 1:h - 1 + tm, cols]
             + cw_ref[2:3, cols] * gs_ref[h:h + tm, cols] + cb_ref[:, cols])
        act = y * (1.0 / (1.0 + jnp.exp(-y))) * us_ref[:, cols]
        o_ref[:, cols] = act.astype(o_ref.dtype)


def _ffn_up(x, gain, wg, wu, conv_w, conv_b, *, seq, tm=1024, tf=512, n_split=2):
    n_rows, d = x.shape
    f = wg.shape[1]
    tm = min(tm, seq)
    assert n_rows % tm == 0 and f % tf == 0 and seq % tm == 0 and tm % CONV_HALO == 0
    assert (tf // n_split) % LANES == 0
    tiles_per_seq = seq // tm
    halo_blocks = tm // CONV_HALO
    kern = functools.partial(_ffn_up_kernel, tm=tm, tiles_per_seq=tiles_per_seq, n_split=n_split)
    return pl.pallas_call(
        kern,
        grid=(n_rows // tm, f // tf),
        in_specs=[
            pl.BlockSpec((tm, d), lambda i, j: (i, 0)),
            pl.BlockSpec((CONV_HALO, d), lambda i, j: (jnp.maximum(i * halo_blocks - 1, 0), 0)),
            pl.BlockSpec((1, d), lambda i, j: (0, 0)),
            pl.BlockSpec((d, tf), lambda i, j: (0, j)),
            pl.BlockSpec((d, tf), lambda i, j: (0, j)),
            pl.BlockSpec((CONV_WIDTH, tf), lambda i, j: (0, j)),
            pl.BlockSpec((1, tf), lambda i, j: (0, j)),
        ],
        out_specs=pl.BlockSpec((tm, tf), lambda i, j: (i, j)),
        out_shape=jax.ShapeDtypeStruct((n_rows, f), BF16),
        scratch_shapes=[pltpu.VMEM((CONV_HALO + tm, d), BF16), pltpu.VMEM((CONV_HALO + tm, tf), F32),
                        pltpu.VMEM((tm, tf), F32)],
        compiler_params=_cparams("parallel", "arbitrary"),
        name="ffn_up",
    )(x, x, gain.reshape(1, d).astype(F32), wg, wu, conv_w.astype(F32), conv_b.reshape(1, f).astype(F32))


def _flash_t_kernel(*refs, tq, ck, hps, n_comp, comp_dk, moba_topk, diff_lambda_init):
    it = iter(refs)
    q_ref, k_ref, v_ref = next(it), next(it), next(it)
    km_ref = next(it) if moba_topk else None
    if diff_lambda_init is not None:
        lq1_ref, lk1_ref, lq2_ref, lk2_ref, gs_ref = (next(it) for _ in range(5))
    o_ref = next(it)
    s_ref, p_ref, a_ref, m_ref, acc_ref = (next(it) for _ in range(5))
    dv = acc_ref.shape[1] - SUM_ROWS
    dq = q_ref.shape[1] // hps
    dk = k_ref.shape[1] // hps
    ns = hps * n_comp

    i = pl.program_id(2)
    m_ref[...] = jnp.full(m_ref.shape, NEG_INF, F32)
    acc_ref[...] = jnp.zeros(acc_ref.shape, F32)

    q_parts = []
    for h in range(hps):
        qt = q_ref[0, h * dq:(h + 1) * dq, :]
        if moba_topk:
            nb = km_ref.shape[0]
            km = km_ref[:, h * dq:(h + 1) * dq]
            km_hi = km.astype(BF16)
            km_lo = (km - km_hi.astype(F32)).astype(BF16)
            gate = _dot(km_hi, qt) + _dot(km_lo, qt)
            row = lax.broadcasted_iota(jnp.int32, gate.shape, 0)
            own = i * (tq // MOBA_BLOCK) + lax.broadcasted_iota(jnp.int32, gate.shape, 1) // MOBA_BLOCK
            neg = jnp.float32(-jnp.inf)
            gate = jnp.where(row < own, gate, neg)
            allowed = row == own
            for _ in range(moba_topk):
                mx = jnp.max(gate, axis=0, keepdims=True)
                first = jnp.min(jnp.where(gate == mx, row, nb), axis=0, keepdims=True)
                pick = (row == first) & (mx > neg)
                allowed = allowed | pick
                gate = jnp.where(pick, neg, gate)
            bias = jnp.where(allowed, 0.0, NEG_INF)
            bias = jnp.concatenate([bias, jnp.zeros((LANES - nb, tq), F32)], axis=0).astype(BF16)
            q_parts.append(jnp.concatenate([qt, bias], axis=0))
        else:
            q_parts += [qt[c * comp_dk:(c + 1) * comp_dk, :] for c in range(n_comp)]

    slabs = ck // VT_CHUNK

    def qk(chunk, slot):
        start = pl.multiple_of(chunk * ck, ck)
        for h in range(hps):
            for c in range(n_comp):
                st = h * n_comp + c
                col = h * dk + c * comp_dk
                s_ref[slot * ns + st] = _dot(k_ref[pl.ds(start, ck), col:col + comp_dk], q_parts[st])

    def sm(chunk, slot, masked):
        for st in range(ns):
            s = s_ref[slot * ns + st]
            if masked:
                key = chunk * ck + lax.broadcasted_iota(jnp.int32, s.shape, 0)
                qry = i * tq + lax.broadcasted_iota(jnp.int32, s.shape, 1)
                s = jnp.where(key <= qry, s, NEG_INF)
            m_prev = m_ref[st]
            m_next = jnp.maximum(m_prev, jnp.max(s, axis=0, keepdims=True))
            m_ref[st] = m_next
            a_ref[slot * ns + st] = jnp.exp2(m_prev - m_next)
            p_ref[slot * ns + st] = jnp.exp2(s - m_next).astype(BF16)

    ones = jnp.ones((SUM_ROWS, ck), BF16)

    def pv(chunk, slot):
        for h in range(hps):
            vt = [v_ref[chunk * slabs + t, h * dv:(h + 1) * dv, :] for t in range(slabs)]
            vt = vt[0] if slabs == 1 else jnp.concatenate(vt, axis=1)
            vt = jnp.concatenate([vt, ones], axis=0)
            for c in range(n_comp):
                st = h * n_comp + c
                acc_ref[st] = acc_ref[st] * a_ref[slot * ns + st] + _dot(vt, p_ref[slot * ns + st])

    n = (i * tq) // ck + 1
    qk(0, 0)

    @pl.when(n >= 2)
    def _():
        sm(0, 0, False)
        qk(1, 1)

    def pair_body(u, carry):
        c0 = 2 * u
        qk(c0 + 2, 0)
        pv(c0, 0)
        sm(c0 + 1, 1, False)
        qk(c0 + 3, 1)
        pv(c0 + 1, 1)
        sm(c0 + 2, 0, False)
        return carry

    pairs = jnp.maximum(n - 2, 0) // 2
    lax.fori_loop(0, pairs, pair_body, 0)

    @pl.when((n % 2 == 1) & (n >= 3))
    def _():
        qk(n - 1, 0)
        pv(n - 3, 0)
        sm(n - 2, 1, False)
        pv(n - 2, 1)

    @pl.when(n % 2 == 1)
    def _():
        sm(n - 1, 0, True)
        pv(n - 1, 0)

    @pl.when(n % 2 == 0)
    def _():
        pv(n - 2, 0)
        sm(n - 1, 1, True)
        pv(n - 1, 1)

    for h in range(hps):
        outs = [acc_ref[h * n_comp + c, 0:dv, :] / acc_ref[h * n_comp + c, dv:dv + 1, :]
                for c in range(n_comp)]
        if diff_lambda_init is None:
            o = outs[0]
        else:
            lam = (jnp.exp(jnp.sum(lq1_ref[...] * lk1_ref[...], axis=1, keepdims=True))
                   - jnp.exp(jnp.sum(lq2_ref[...] * lk2_ref[...], axis=1, keepdims=True))
                   + diff_lambda_init)
            o = outs[0] - lam * outs[1]
            ms = jnp.mean(o * o, axis=0, keepdims=True)
            o = o * lax.rsqrt(ms + NORM_EPS) * _rep(gs_ref[...], tq // LANES) * (1.0 - diff_lambda_init)
        o_ref[:, h * dv:(h + 1) * dv] = o.T.astype(o_ref.dtype)


def _flash_t(q_t, k, v_t, *, batch, seq, heads, tq, ck, dk, dv, name, hps=1, n_comp=1, kmean=None, diff=None):
    assert seq % ck == 0 and ck % tq == 0 and tq % VT_CHUNK == 0 and heads % hps == 0
    nq = seq // tq
    dq = q_t.shape[1] // heads
    assert q_t.shape == (batch * nq, heads * dq, tq) and k.shape == (batch * seq, heads * dk)
    assert v_t.shape == (batch * seq // VT_CHUNK, heads * dv, VT_CHUNK)
    slabs_per_seq = seq // VT_CHUNK
    in_specs = [
        pl.BlockSpec((1, hps * dq, tq), lambda b, h, i: (b * nq + i, h, 0)),
        pl.BlockSpec((seq, hps * dk), lambda b, h, i: (b, h)),
        pl.BlockSpec((slabs_per_seq, hps * dv, VT_CHUNK), lambda b, h, i: (b, h, 0)),
    ]
    args = [q_t, k, v_t]
    moba_topk = 0
    if kmean is not None:
        nb = seq // MOBA_BLOCK
        assert tq % MOBA_BLOCK == 0 and nb <= LANES and dq + LANES == dk
        moba_topk = min(MOBA_TOPK, nb)
        in_specs.append(pl.BlockSpec((nb, hps * dq), lambda b, h, i: (b, h)))
        args.append(kmean)
    lambda_init = None
    if diff is not None:
        lq1, lk1, lq2, lk2, g_sub, lambda_init = diff
        for a in (lq1, lk1, lq2, lk2):
            in_specs.append(pl.BlockSpec((1, a.shape[0]), lambda b, h, i: (0, 0)))
            args.append(a.reshape(1, -1).astype(F32))
        in_specs.append(pl.BlockSpec((dv, LANES), lambda b, h, i: (0, 0)))
        args.append(jnp.broadcast_to(g_sub.astype(F32)[:, None], (dv, LANES)))
    kern = functools.partial(_flash_t_kernel, tq=tq, ck=ck, hps=hps, n_comp=n_comp, comp_dk=dk // n_comp,
                             moba_topk=moba_topk, diff_lambda_init=lambda_init)
    ns = hps * n_comp
    scratch = [
        pltpu.VMEM((2 * ns, ck, tq), F32),
        pltpu.VMEM((2 * ns, ck, tq), BF16),
        pltpu.VMEM((2 * ns, 1, tq), F32),
        pltpu.VMEM((ns, 1, tq), F32),
        pltpu.VMEM((ns, dv + SUM_ROWS, tq), F32),
    ]
    return pl.pallas_call(
        kern,
        grid=(batch, heads // hps, nq),
        in_specs=in_specs,
        out_specs=pl.BlockSpec((tq, hps * dv), lambda b, h, i: (b * nq + i, h)),
        out_shape=jax.ShapeDtypeStruct((batch * seq, heads * dv), BF16),
        scratch_shapes=scratch,
        compiler_params=_cparams("parallel", "parallel", "arbitrary"),
        name=name,
    )(*args)


def _swa_kernel(sink_ref, q_ref, k_ref, v_ref, o_ref, *, tq, window, group):
    kh = pl.program_id(1)
    i = pl.program_id(2)
    half = LANES // 2
    n_sub = tq // window
    pairs = group // 2

    def sub(sb, carry):
        r0 = i * tq + sb * window
        k0 = jnp.maximum(r0 - window, 0)
        qrow = sb * window
        kstart = pl.multiple_of(k0, window)
        kk = k_ref[pl.ds(kstart, 2 * window), :]
        vv = v_ref[pl.ds(kstart, 2 * window), :]
        qpos = r0 + lax.broadcasted_iota(jnp.int32, (window, 2 * window), 0)
        kpos = k0 + lax.broadcasted_iota(jnp.int32, (window, 2 * window), 1)
        valid = (kpos <= qpos) & (kpos > qpos - window)
        lane_lo = lax.broadcasted_iota(jnp.int32, (window, LANES), 1) < half
        for pr in range(pairs):
            q2 = q_ref[pl.ds(qrow, window), pr * LANES:(pr + 1) * LANES]
            outs = []
            for e in range(2):
                keep = lane_lo if e == 0 else jnp.logical_not(lane_lo)
                qe = jnp.where(keep, q2, jnp.zeros_like(q2))
                s = jnp.where(valid, _dot_t(qe, kk), NEG_INF)
                sink = sink_ref[kh * group + 2 * pr + e]
                m = jnp.maximum(jnp.max(s, axis=1, keepdims=True), sink)
                p = jnp.exp(s - m)
                denom = jnp.sum(p, axis=1, keepdims=True) + jnp.exp(sink - m)
                p = p / denom
                outs.append(_dot(p.astype(BF16), vv))
            o_pair = jnp.where(lane_lo, outs[0], outs[1])
            o_ref[pl.ds(qrow, window), pr * LANES:(pr + 1) * LANES] = o_pair.astype(o_ref.dtype)
        return carry

    for sb in range(n_sub):
        sub(sb, 0)


def _swa_attention(q, k, v, sinks, *, batch, seq, kv_heads, tq=512):
    tq = min(tq, seq)
    assert seq % tq == 0 and tq % SWA_WINDOW == 0 and seq >= 2 * SWA_WINDOW
    nq = seq // tq
    gw = SWA_GROUP * SWA_HEAD_DIM
    kern = functools.partial(_swa_kernel, tq=tq, window=SWA_WINDOW, group=SWA_GROUP)
    return pl.pallas_call(
        kern,
        grid=(batch, kv_heads, nq),
        in_specs=[
            pl.BlockSpec(memory_space=pltpu.SMEM),
            pl.BlockSpec((tq, gw), lambda b, h, i: (b * nq + i, h)),
            pl.BlockSpec((seq, LANES), lambda b, h, i: (b, h)),
            pl.BlockSpec((seq, LANES), lambda b, h, i: (b, h)),
        ],
        out_specs=pl.BlockSpec((tq, gw), lambda b, h, i: (b * nq + i, h)),
        out_shape=jax.ShapeDtypeStruct((batch * seq, kv_heads * gw), BF16),
        compiler_params=_cparams("parallel", "parallel", "arbitrary"),
        name="swa_attn",
    )(sinks.astype(F32), q, k, v)


def _tile_gain(g, n):
    return jnp.tile(g.astype(F32), n)


def _moba_layer(x, an, wq, wk, wv, gq, gk, wo, *, batch, seq):
    dh, blk = MOBA_HEAD_DIM, MOBA_BLOCK
    heads = wq.shape[1] // dh
    rot = dh // ROT_FRACTION
    rope = _rope_tables(seq, LANES, rot)
    tiles_per_seq = seq // 512
    tq = min(MOBA_QUERY_TILE, seq)
    q_t = _norm_proj_t(x, an, wq.T.astype(BF16), pattern=[("norm", dh, True)], seq=seq, pos_chunk=tq,
                       hg=_tile_gain(gq, heads), rope=rope, rope_half=rot // 2, out_scale=dh ** -0.5 * LOG2E,
                       name="moba_q_proj")
    block_id = jnp.arange(seq, dtype=jnp.int32)[:, None] // blk
    onehot = (block_id == jnp.arange(LANES, dtype=jnp.int32)[None, :]).astype(F32)
    k, kmean = _norm_proj(x, an, wk.astype(BF16), pattern=[("norm", dh, False, True)], seq=seq,
                          hg=_tile_gain(gk, heads), rope=rope, rope_half=rot // 2, kmean_rows=blk,
                          pair=(onehot, lambda i, j: (i % tiles_per_seq, 0)), name="moba_k_proj")
    v_t = _norm_proj_t(x, an, wv.T.astype(BF16), pattern=[("copy",)], seq=seq, pos_chunk=VT_CHUNK,
                       name="moba_v_proj")
    kmean = kmean.reshape(kmean.shape[0], kmean.shape[2])
    o = _flash_t(q_t, k, v_t, batch=batch, seq=seq, heads=heads, tq=tq, ck=min(MOBA_KEY_CHUNK, seq), dk=2 * dh,
                 dv=dh, hps=ATTN_HEADS_PER_STEP, kmean=kmean, name="moba_attn")
    return _matmul_residual(o, wo.astype(BF16), x, name="moba_out_proj")


def _mla_layer(x, an, wq_a, g_qa, wq_b, wkv_a, g_kva, wkv_b, g_qn, g_kn, g_qr, g_kr, wo, *, batch, seq):
    d = x.shape[1]
    nope, rd, vd, pad = MLA_NOPE_DIM, MLA_ROPE_DIM, MLA_V_DIM, MLA_QK_PAD
    q_rank = wq_a.shape[1]
    kv_rank = wkv_a.shape[1] - rd
    heads = wq_b.shape[1] // (nope + rd)
    rope = _rope_tables(seq, LANES, rd)
    zpad = lambda n: jnp.zeros((n,), F32)
    tq = min(512, seq)

    w_a = jnp.concatenate([wq_a, wkv_a, jnp.zeros((d, LANES - rd), F32)], axis=1).astype(BF16)
    n_a = w_a.shape[1]
    chunks_a = n_a // LANES
    hg_a = jnp.concatenate([jnp.ones((q_rank + kv_rank,), F32), g_kr.astype(F32), zpad(LANES - rd)])
    pat_a = [("copy",)] * (chunks_a - 1) + [("norm", rd, False, True)]
    lat = _norm_proj(x, an, w_a, pattern=pat_a, seq=seq, hg=hg_a, rope=rope, rope_half=rd // 2,
                     out_dtype=F32, name="mla_latent_proj")

    wq_b3 = wq_b.reshape(q_rank, heads, nope + rd)
    wq_p = jnp.concatenate([wq_b3, jnp.zeros((q_rank, heads, pad - nope - rd), F32)], axis=2)
    wq_p = wq_p.reshape(q_rank, heads * pad)
    hg_q = jnp.tile(jnp.concatenate([g_qn.astype(F32), g_qr.astype(F32), zpad(pad - nope - rd)]), heads)
    pat_q = [("norm", nope, False), ("norm", rd, True)]
    q_t = _norm_proj_t(lat, g_qa, wq_p.T.astype(BF16), pattern=pat_q, seq=seq, pos_chunk=tq, xcol=0, hg=hg_q,
                       rope=rope, rope_half=rd // 2, out_scale=(nope + rd) ** -0.5 * LOG2E, name="mla_q_proj")

    wkv3 = wkv_b.reshape(kv_rank, heads, nope + vd)
    w_k = wkv3[:, :, :nope].reshape(kv_rank, heads * nope).astype(BF16)
    w_v = wkv3[:, :, nope:].reshape(kv_rank, heads * vd)
    assert q_rank == kv_rank
    kr_col = (q_rank + kv_rank) // LANES
    k = _norm_proj(lat, g_kva, w_k, pattern=[("norm", nope, False, False)], seq=seq, xcol=1,
                   hg=_tile_gain(g_kn, heads), pair=(lat, lambda i, j: (i, kr_col)), name="mla_k_proj")
    v_t = _norm_proj_t(lat, g_kva, w_v.T.astype(BF16), pattern=[("copy",)], seq=seq, pos_chunk=VT_CHUNK, xcol=1,
                       name="mla_v_proj")
    o = _flash_t(q_t, k, v_t, batch=batch, seq=seq, heads=heads, tq=tq, ck=tq, dk=pad, dv=vd,
                 hps=ATTN_HEADS_PER_STEP, name="mla_attn")
    return _matmul_residual(o, wo.astype(BF16), x, name="mla_out_proj")


def _swa_layer(x, an, wq, wk, wv, gq, gk, sinks, wo, *, batch, seq):
    d = x.shape[1]
    dh = SWA_HEAD_DIM
    hq = wq.shape[1] // dh
    hkv = wk.shape[1] // dh
    rot = dh // ROT_FRACTION
    rope = _rope_tables(seq, dh, rot)

    def dup(w):
        w3 = w.reshape(d, hkv, dh)
        return jnp.concatenate([w3, w3], axis=2).reshape(d, hkv * 2 * dh).astype(BF16)

    q = _norm_proj(x, an, wq.astype(BF16), pattern=[("norm", dh, True, True)], seq=seq, hg=_tile_gain(gq, hq),
                   rope=rope, rope_half=rot // 2, out_scale=dh ** -0.5, name="swa_q_proj")
    k = _norm_proj(x, an, dup(wk), pattern=[("norm", 2 * dh, False, True)], seq=seq,
                   hg=_tile_gain(gk, 2 * hkv), rope=rope, rope_half=rot // 2, name="swa_k_proj")
    v = _norm_proj(x, an, dup(wv), pattern=[("copy",)], seq=seq, name="swa_v_proj")
    o = _swa_attention(q, k, v, sinks, batch=batch, seq=seq, kv_heads=hkv)
    return _matmul_residual(o, wo.astype(BF16), x, name="swa_out_proj")


def _diff_layer(x, an, wq, wk, wv, gq, gk, lq1, lk1, lq2, lk2, g_sub, wo, lambda_init, *, batch, seq):
    dh = DIFF_HEAD_DIM
    heads = wq.shape[1] // (2 * dh)
    rot = dh // ROT_FRACTION
    rope = _rope_tables(seq, LANES, rot)
    tq = min(512, seq)
    q_t = _norm_proj_t(x, an, wq.T.astype(BF16), pattern=[("norm", dh, True)], seq=seq, pos_chunk=tq,
                       hg=_tile_gain(gq, 2 * heads), rope=rope, rope_half=rot // 2, out_scale=dh ** -0.5 * LOG2E,
                       name="diff_q_proj")
    k = _norm_proj(x, an, wk.astype(BF16), pattern=[("norm", dh, False, True)], seq=seq,
                   hg=_tile_gain(gk, 2 * heads), rope=rope, rope_half=rot // 2, name="diff_k_proj")
    v_t = _norm_proj_t(x, an, wv.T.astype(BF16), pattern=[("copy",)], seq=seq, pos_chunk=VT_CHUNK,
                       name="diff_v_proj")
    o = _flash_t(q_t, k, v_t, batch=batch, seq=seq, heads=heads, tq=tq, ck=tq, dk=2 * dh, dv=2 * dh, n_comp=2,
                 diff=(lq1, lk1, lq2, lk2, g_sub, lambda_init), name="diff_attn")
    return _matmul_residual(o, wo.astype(BF16), x, name="diff_out_proj")


def _ffn_layer(x, fn, w_gate, w_up, conv_w, conv_b, w_down, *, seq):
    act = _ffn_up(x, fn, w_gate.astype(BF16), w_up.astype(BF16), conv_w, conv_b, seq=seq)
    return _matmul_residual(act, w_down.astype(BF16), x, name="ffn_down")


def kernel(x, attn_norm, ffn_norm, moba_wq, moba_wk, moba_wv, moba_gq, moba_gk, moba_wo, mla_wq_a, mla_g_qa, mla_wq_b, mla_wkv_a, mla_g_kva, mla_wkv_b, mla_g_qn, mla_g_kn, mla_g_qr, mla_g_kr, mla_wo, swa_wq, swa_wk, swa_wv, swa_gq, swa_gk, swa_sinks, swa_wo, diff_wq, diff_wk, diff_wv, diff_gq, diff_gk, diff_lq1, diff_lk1, diff_lq2, diff_lk2, diff_g_sub, diff_wo, ffn_w_gate, ffn_w_up, ffn_conv_w, ffn_conv_b, ffn_w_down):
    batch, seq, d = x.shape
    h = x.reshape(batch * seq, d)
    kw = dict(batch=batch, seq=seq)
    for i in range(DEPTH):
        m, j = i % 4, i // 4
        if m == 0:
            h = _moba_layer(h, attn_norm[i], moba_wq[j], moba_wk[j], moba_wv[j], moba_gq[j], moba_gk[j],
                            moba_wo[j], **kw)
        elif m == 1:
            h = _mla_layer(h, attn_norm[i], mla_wq_a[j], mla_g_qa[j], mla_wq_b[j], mla_wkv_a[j], mla_g_kva[j],
                           mla_wkv_b[j], mla_g_qn[j], mla_g_kn[j], mla_g_qr[j], mla_g_kr[j], mla_wo[j], **kw)
        elif m == 2:
            h = _swa_layer(h, attn_norm[i], swa_wq[j], swa_wk[j], swa_wv[j], swa_gq[j], swa_gk[j],
                           swa_sinks[j], swa_wo[j], **kw)
        else:
            lambda_init = 0.8 - 0.6 * math.exp(-0.3 * i)
            h = _diff_layer(h, attn_norm[i], diff_wq[j], diff_wk[j], diff_wv[j], diff_gq[j], diff_gk[j],
                            diff_lq1[j], diff_lk1[j], diff_lq2[j], diff_lk2[j], diff_g_sub[j], diff_wo[j],
                            lambda_init, **kw)
        h = _ffn_layer(h, ffn_norm[i], ffn_w_gate[i], ffn_w_up[i], ffn_conv_w[i], ffn_conv_b[i],
                       ffn_w_down[i], seq=seq)
    return h.reshape(batch, seq, d)
```

```python
import functools
import math

import jax
import jax.numpy as jnp
from jax import lax
from jax.experimental import pallas as pl
from jax.experimental.pallas import tpu as pltpu

F32 = jnp.float32
BF16 = jnp.bfloat16

LANES = 128
NORM_EPS = 1e-6
NEG_INF = -1e30
ROPE_THETA = 500000.0
ROT_FRACTION = 4
DEPTH = 4

MOBA_HEAD_DIM = 128
MOBA_BLOCK = 256
MOBA_TOPK = 3
MOBA_KEY_CHUNK = 512
MOBA_QUERY_TILE = 512

MLA_NOPE_DIM = 128
MLA_ROPE_DIM = 64
MLA_V_DIM = 128
MLA_QK_PAD = 256

SWA_HEAD_DIM = 64
SWA_GROUP = 8
SWA_WINDOW = 128

DIFF_HEAD_DIM = 128

CONV_WIDTH = 3
CONV_HALO = 16

VT_CHUNK = 256
SUM_ROWS = 16
ATTN_HEADS_PER_STEP = 2
LOG2E = math.log2(math.e)
MXU_RESIDENT_BYTES = 8 * 1024 * 1024
PROJ_GROUP = 2
VMEM_LIMIT = 56 * 1024 * 1024


def _cparams(*sem):
    return pltpu.CompilerParams(dimension_semantics=sem, vmem_limit_bytes=VMEM_LIMIT)


def _dot(a, b):
    return jnp.dot(a, b, preferred_element_type=F32)


def _dot_t(a, b):
    return lax.dot_general(a, b, (((1,), (1,)), ((), ())), preferred_element_type=F32)


def _rep(x, n):
    return x if n == 1 else jnp.concatenate([x] * n, axis=1)


def _rope_tables(seq, seg, rot):
    half = rot // 2
    inv = ROPE_THETA ** (-jnp.arange(half, dtype=F32) / half)
    ang = jnp.arange(seq, dtype=jnp.int32).astype(F32)[:, None] * inv[None, :]
    cos, sin = jnp.cos(ang), jnp.sin(ang)
    ones = jnp.ones((seq, seg - rot), F32)
    zeros_h = jnp.zeros((seq, half), F32)
    zeros_r = jnp.zeros((seq, seg - rot), F32)
    c = jnp.concatenate([cos, cos, ones], axis=1)
    s1 = jnp.concatenate([-sin, zeros_h, zeros_r], axis=1)
    s2 = jnp.concatenate([zeros_h, sin, zeros_r], axis=1)
    reps = LANES // seg
    return tuple(jnp.tile(t, (1, reps)) for t in (c, s1, s2))


def _rms_rows_to_bf16(x_ref, g_ref, xn_ref):
    x = x_ref[...]
    ms = jnp.mean(x * x, axis=-1, keepdims=True)
    xn_ref[...] = (x * lax.rsqrt(ms + NORM_EPS) * g_ref[...]).astype(BF16)


def _proj_kernel(*refs, pattern, rope_half, out_scale, n_chunks, has_rope, has_hg, kmean_rows, has_pair):
    it = iter(refs)
    x_ref, g_ref, w_ref = next(it), next(it), next(it)
    hg_ref = next(it) if has_hg else None
    if has_rope:
        c_ref, s1_ref, s2_ref = next(it), next(it), next(it)
    pair_ref = next(it) if has_pair else None
    o_ref = next(it)
    km_ref = next(it) if kmean_rows else None
    xn_ref, acc_ref = next(it), next(it)

    @pl.when(pl.program_id(1) == 0)
    def _():
        _rms_rows_to_bf16(x_ref, g_ref, xn_ref)

    for c in range(0, n_chunks, PROJ_GROUP):
        cols = slice(c * LANES, min(c + PROJ_GROUP, n_chunks) * LANES)
        acc_ref[:, cols] = _dot(xn_ref[...], w_ref[:, cols])
    for c in range(n_chunks):
        mode = pattern[c % len(pattern)]
        blk = acc_ref[:, c * LANES:(c + 1) * LANES]
        if mode[0] == "norm":
            _, width, two, rope = mode
            sq = blk * blk
            if two:
                lo = lax.broadcasted_iota(jnp.int32, blk.shape, 1) < (LANES // 2)
                s_lo = jnp.sum(jnp.where(lo, sq, 0.0), axis=1, keepdims=True)
                s_hi = jnp.sum(jnp.where(lo, 0.0, sq), axis=1, keepdims=True)
                r = jnp.where(lo, lax.rsqrt(s_lo * (1.0 / width) + NORM_EPS),
                              lax.rsqrt(s_hi * (1.0 / width) + NORM_EPS))
            else:
                r = lax.rsqrt(jnp.sum(sq, axis=1, keepdims=True) * (1.0 / width) + NORM_EPS)
            blk = blk * r * hg_ref[:, c * LANES:(c + 1) * LANES]
            if rope:
                blk = (blk * c_ref[...]
                       + pltpu.roll(blk, LANES - rope_half, 1) * s1_ref[...]
                       + pltpu.roll(blk, rope_half, 1) * s2_ref[...])
            if out_scale != 1.0:
                blk = blk * out_scale
        if has_pair:
            o_ref[:, (2 * c) * LANES:(2 * c + 1) * LANES] = blk.astype(o_ref.dtype)
            o_ref[:, (2 * c + 1) * LANES:(2 * c + 2) * LANES] = pair_ref[...].astype(o_ref.dtype)
        else:
            o_ref[:, c * LANES:(c + 1) * LANES] = blk.astype(o_ref.dtype)
        if kmean_rows:
            for r_i in range(blk.shape[0] // kmean_rows):
                part = blk[r_i * kmean_rows:(r_i + 1) * kmean_rows, :]
                km_ref[r_i:r_i + 1, :, c * LANES:(c + 1) * LANES] = (
                    jnp.sum(part, axis=0, keepdims=True) * (1.0 / kmean_rows))[None]


def _norm_proj(x, gain, w, *, pattern, seq, name, xcol=0, hg=None, rope=None, rope_half=0, out_scale=1.0,
               out_dtype=None, kmean_rows=0, pair=None, tm=512):
    n_rows = x.shape[0]
    k_dim, n_out = w.shape
    assert k_dim * n_out * w.dtype.itemsize <= MXU_RESIDENT_BYTES
    tn = n_out
    out_dtype = out_dtype or BF16
    assert n_rows % tm == 0 and n_out % tn == 0 and seq % tm == 0
    n_chunks = tn // LANES
    assert n_chunks % len(pattern) == 0
    tiles_per_seq = seq // tm
    in_specs = [
        pl.BlockSpec((tm, k_dim), lambda i, j: (i, xcol)),
        pl.BlockSpec((1, k_dim), lambda i, j: (0, 0)),
        pl.BlockSpec((k_dim, tn), lambda i, j: (0, j)),
    ]
    args = [x, gain.reshape(1, k_dim).astype(F32), w]
    if hg is not None:
        in_specs.append(pl.BlockSpec((1, tn), lambda i, j: (0, j)))
        args.append(hg.reshape(1, n_out).astype(F32) * out_scale)
        out_scale = 1.0
    if rope is not None:
        for t in rope:
            in_specs.append(pl.BlockSpec((tm, LANES), lambda i, j: (i % tiles_per_seq, 0)))
            args.append(t)
    out_mult = 1
    if pair is not None:
        pair_arr, pair_map = pair
        in_specs.append(pl.BlockSpec((tm, LANES), pair_map))
        args.append(pair_arr)
        out_mult = 2
    out_shape = [jax.ShapeDtypeStruct((n_rows, n_out * out_mult), out_dtype)]
    out_specs = [pl.BlockSpec((tm, tn * out_mult), lambda i, j: (i, j))]
    if kmean_rows:
        out_shape.append(jax.ShapeDtypeStruct((n_rows // kmean_rows, 1, n_out), F32))
        out_specs.append(pl.BlockSpec((tm // kmean_rows, 1, tn), lambda i, j: (i, 0, j)))
    kern = functools.partial(
        _proj_kernel, pattern=tuple(pattern), rope_half=rope_half, out_scale=out_scale, n_chunks=n_chunks,
        has_rope=rope is not None, has_hg=hg is not None, kmean_rows=kmean_rows, has_pair=pair is not None)
    res = pl.pallas_call(
        kern,
        grid=(n_rows // tm, n_out // tn),
        in_specs=in_specs,
        out_specs=out_specs,
        out_shape=out_shape,
        scratch_shapes=[pltpu.VMEM((tm, k_dim), BF16), pltpu.VMEM((tm, tn), F32)],
        compiler_params=_cparams("parallel", "arbitrary"),
        name=name,
    )(*args)
    return res if kmean_rows else res[0]


def _proj_t_kernel(*refs, pattern, rope_half, n_chunks, has_rope, has_hg, pos_chunk):
    it = iter(refs)
    x_ref, g_ref, w_ref = next(it), next(it), next(it)
    hg_ref = next(it) if has_hg else None
    if has_rope:
        c_ref, s_ref = next(it), next(it)
    o_ref = next(it)
    xn_ref, acc_ref = next(it), next(it)

    @pl.when(pl.program_id(1) == 0)
    def _():
        _rms_rows_to_bf16(x_ref, g_ref, xn_ref)

    tm = xn_ref.shape[0]
    for c in range(0, n_chunks, PROJ_GROUP):
        rows = slice(c * LANES, min(c + PROJ_GROUP, n_chunks) * LANES)
        acc_ref[rows, :] = _dot_t(w_ref[rows, :], xn_ref[...])
    for c in range(n_chunks):
        mode = pattern[c % len(pattern)]
        blk = acc_ref[c * LANES:(c + 1) * LANES, :]
        if mode[0] == "norm":
            _, width, rope = mode
            r = lax.rsqrt(jnp.sum(blk * blk, axis=0, keepdims=True) * (1.0 / width) + NORM_EPS)
            blk = blk * r * _rep(hg_ref[c * LANES:(c + 1) * LANES, :], tm // LANES)
            if rope:
                rot = 2 * rope_half
                top = blk[0:rot, :]
                swapped = jnp.concatenate([top[rope_half:, :], top[:rope_half, :]], axis=0)
                blk = jnp.concatenate([top * c_ref[...] + swapped * s_ref[...], blk[rot:, :]], axis=0)
        for pc in range(tm // pos_chunk):
            o_ref[pc, c * LANES:(c + 1) * LANES, :] = blk[:, pc * pos_chunk:(pc + 1) * pos_chunk].astype(o_ref.dtype)


def _norm_proj_t(x, gain, w_t, *, pattern, seq, name, pos_chunk, xcol=0, hg=None, rope=None, rope_half=0,
                 out_scale=1.0, tm=512):
    n_rows = x.shape[0]
    n_out, k_dim = w_t.shape
    assert k_dim * n_out * w_t.dtype.itemsize <= MXU_RESIDENT_BYTES
    tn = n_out
    assert n_rows % tm == 0 and n_out % tn == 0 and seq % tm == 0 and tm % pos_chunk == 0
    n_chunks = tn // LANES
    assert n_chunks % len(pattern) == 0
    tiles_per_seq = seq // tm
    in_specs = [
        pl.BlockSpec((tm, k_dim), lambda i, j: (i, xcol)),
        pl.BlockSpec((1, k_dim), lambda i, j: (0, 0)),
        pl.BlockSpec((tn, k_dim), lambda i, j: (j, 0)),
    ]
    args = [x, gain.reshape(1, k_dim).astype(F32), w_t]
    assert hg is not None or out_scale == 1.0
    if hg is not None:
        in_specs.append(pl.BlockSpec((tn, LANES), lambda i, j: (j, 0)))
        args.append(jnp.broadcast_to((hg.astype(F32) * out_scale)[:, None], (n_out, LANES)))
    if rope is not None:
        rot = 2 * rope_half
        c_tab, s1_tab, s2_tab = rope
        for t in (c_tab, s1_tab + s2_tab):
            in_specs.append(pl.BlockSpec((rot, tm), lambda i, j: (0, i % tiles_per_seq)))
            args.append(t.T)
    kern = functools.partial(
        _proj_t_kernel, pattern=tuple(pattern), rope_half=rope_half, n_chunks=n_chunks,
        has_rope=rope is not None, has_hg=hg is not None, pos_chunk=pos_chunk)
    per_tile = tm // pos_chunk
    return pl.pallas_call(
        kern,
        grid=(n_rows // tm, n_out // tn),
        in_specs=in_specs,
        out_specs=pl.BlockSpec((per_tile, tn, pos_chunk), lambda i, j: (i, j, 0)),
        out_shape=jax.ShapeDtypeStruct((n_rows // pos_chunk, n_out, pos_chunk), BF16),
        scratch_shapes=[pltpu.VMEM((tm, k_dim), BF16), pltpu.VMEM((tn, tm), F32)],
        compiler_params=_cparams("parallel", "arbitrary"),
        name=name,
    )(*args)


def _mm_res_kernel(a_ref, w_ref, r_ref, o_ref):
    o_ref[...] = r_ref[...] + _dot(a_ref[...], w_ref[...])


def _matmul_residual(a, w, res, *, name):
    n_rows, k_dim = a.shape
    n_out = w.shape[1]
    if k_dim * n_out * w.dtype.itemsize <= MXU_RESIDENT_BYTES:
        tm, tn = 512, n_out
    else:
        tm, tn = 1024, 512
    tm = min(tm, n_rows)
    assert n_rows % tm == 0 and n_out % tn == 0
    return pl.pallas_call(
        _mm_res_kernel,
        grid=(n_rows // tm, n_out // tn),
        in_specs=[
            pl.BlockSpec((tm, k_dim), lambda i, j: (i, 0)),
            pl.BlockSpec((k_dim, tn), lambda i, j: (0, j)),
            pl.BlockSpec((tm, tn), lambda i, j: (i, j)),
        ],
        out_specs=pl.BlockSpec((tm, tn), lambda i, j: (i, j)),
        out_shape=jax.ShapeDtypeStruct((n_rows, n_out), F32),
        compiler_params=_cparams("parallel", "parallel"),
        name=name,
    )(a, w, res)


def _ffn_up_kernel(x_ref, xh_ref, g_ref, wg_ref, wu_ref, cw_ref, cb_ref, o_ref, xn_ref, gs_ref, us_ref,
                   *, tm, tiles_per_seq, n_split):
    i = pl.program_id(0)
    h = CONV_HALO

    @pl.when(pl.program_id(1) == 0)
    def _():
        _rms_rows_to_bf16(xh_ref, g_ref, xn_ref.at[0:h, :])
        _rms_rows_to_bf16(x_ref, g_ref, xn_ref.at[h:h + tm, :])

    seq_start = (i % tiles_per_seq) == 0
    tf = o_ref.shape[1]
    w = tf // n_split
    for s in range(n_split):
        cols = slice(s * w, (s + 1) * w)
        g = _dot(xn_ref[...], wg_ref[:, cols])
        gs_ref[0:h, cols] = jnp.where(seq_start, 0.0, g[0:h, :])
        gs_ref[h:h + tm, cols] = g[h:h + tm, :]
    for s in range(n_split):
        cols = slice(s * w, (s + 1) * w)
        us_ref[:, cols] = _dot(xn_ref[h:h + tm, :], wu_ref[:, cols])
    for s in range(n_split):
        cols = slice(s * w, (s + 1) * w)
        y = (cw_ref[0:1, cols] * gs_ref[h - 2:h - 2 + tm, cols]
             + cw_ref[1:2, cols] * gs_ref[h - 1:h - 1 + tm, cols]
             + cw_ref[2:3, cols] * gs_ref[h:h + tm, cols] + cb_ref[:, cols])
        act = y * (1.0 / (1.0 + jnp.exp(-y))) * us_ref[:, cols]
        o_ref[:, cols] = act.astype(o_ref.dtype)


def _ffn_up(x, gain, wg, wu, conv_w, conv_b, *, seq, tm=1024, tf=512, n_split=2):
    n_rows, d = x.shape
    f = wg.shape[1]
    tm = min(tm, seq)
    assert n_rows % tm == 0 and f % tf == 0 and seq % tm == 0 and tm % CONV_HALO == 0
    assert (tf // n_split) % LANES == 0
    tiles_per_seq = seq // tm
    halo_blocks = tm // CONV_HALO
    kern = functools.partial(_ffn_up_kernel, tm=tm, tiles_per_seq=tiles_per_seq, n_split=n_split)
    return pl.pallas_call(
        kern,
        grid=(n_rows // tm, f // tf),
        in_specs=[
            pl.BlockSpec((tm, d), lambda i, j: (i, 0)),
            pl.BlockSpec((CONV_HALO, d), lambda i, j: (jnp.maximum(i * halo_blocks - 1, 0), 0)),
            pl.BlockSpec((1, d), lambda i, j: (0, 0)),
            pl.BlockSpec((d, tf), lambda i, j: (0, j)),
            pl.BlockSpec((d, tf), lambda i, j: (0, j)),
            pl.BlockSpec((CONV_WIDTH, tf), lambda i, j: (0, j)),
            pl.BlockSpec((1, tf), lambda i, j: (0, j)),
        ],
        out_specs=pl.BlockSpec((tm, tf), lambda i, j: (i, j)),
        out_shape=jax.ShapeDtypeStruct((n_rows, f), BF16),
        scratch_shapes=[pltpu.VMEM((CONV_HALO + tm, d), BF16), pltpu.VMEM((CONV_HALO + tm, tf), F32),
                        pltpu.VMEM((tm, tf), F32)],
        compiler_params=_cparams("parallel", "arbitrary"),
        name="ffn_up",
    )(x, x, gain.reshape(1, d).astype(F32), wg, wu, conv_w.astype(F32), conv_b.reshape(1, f).astype(F32))


def _flash_t_kernel(*refs, tq, ck, hps, n_comp, comp_dk, moba_topk, diff_lambda_init):
    it = iter(refs)
    q_ref, k_ref, v_ref = next(it), next(it), next(it)
    km_ref = next(it) if moba_topk else None
    if diff_lambda_init is not None:
        lq1_ref, lk1_ref, lq2_ref, lk2_ref, gs_ref = (next(it) for _ in range(5))
    o_ref = next(it)
    s_ref, p_ref, a_ref, m_ref, acc_ref = (next(it) for _ in range(5))
    dv = acc_ref.shape[1] - SUM_ROWS
    dq = q_ref.shape[1] // hps
    dk = k_ref.shape[1] // hps
    ns = hps * n_comp

    i = pl.program_id(2)
    m_ref[...] = jnp.full(m_ref.shape, NEG_INF, F32)
    acc_ref[...] = jnp.zeros(acc_ref.shape, F32)

    q_parts = []
    for h in range(hps):
        qt = q_ref[0, h * dq:(h + 1) * dq, :]
        if moba_topk:
            nb = km_ref.shape[0]
            km = km_ref[:, h * dq:(h + 1) * dq]
            km_hi = km.astype(BF16)
            km_lo = (km - km_hi.astype(F32)).astype(BF16)
            gate = _dot(km_hi, qt) + _dot(km_lo, qt)
            row = lax.broadcasted_iota(jnp.int32, gate.shape, 0)
            own = i * (tq // MOBA_BLOCK) + lax.broadcasted_iota(jnp.int32, gate.shape, 1) // MOBA_BLOCK
            neg = jnp.float32(-jnp.inf)
            gate = jnp.where(row < own, gate, neg)
            allowed = row == own
            for _ in range(moba_topk):
                mx = jnp.max(gate, axis=0, keepdims=True)
                first = jnp.min(jnp.where(gate == mx, row, nb), axis=0, keepdims=True)
                pick = (row == first) & (mx > neg)
                allowed = allowed | pick
                gate = jnp.where(pick, neg, gate)
            bias = jnp.where(allowed, 0.0, NEG_INF)
            bias = jnp.concatenate([bias, jnp.zeros((LANES - nb, tq), F32)], axis=0).astype(BF16)
            q_parts.append(jnp.concatenate([qt, bias], axis=0))
        else:
            q_parts += [qt[c * comp_dk:(c + 1) * comp_dk, :] for c in range(n_comp)]

    slabs = ck // VT_CHUNK

    def qk(chunk, slot):
        start = pl.multiple_of(chunk * ck, ck)
        for h in range(hps):
            for c in range(n_comp):
                st = h * n_comp + c
                col = h * dk + c * comp_dk
                s_ref[slot * ns + st] = _dot(k_ref[pl.ds(start, ck), col:col + comp_dk], q_parts[st])

    def sm(chunk, slot, masked):
        for st in range(ns):
            s = s_ref[slot * ns + st]
            if masked:
                key = chunk * ck + lax.broadcasted_iota(jnp.int32, s.shape, 0)
                qry = i * tq + lax.broadcasted_iota(jnp.int32, s.shape, 1)
                s = jnp.where(key <= qry, s, NEG_INF)
            m_prev = m_ref[st]
            m_next = jnp.maximum(m_prev, jnp.max(s, axis=0, keepdims=True))
            m_ref[st] = m_next
            a_ref[slot * ns + st] = jnp.exp2(m_prev - m_next)
            p_ref[slot * ns + st] = jnp.exp2(s - m_next).astype(BF16)

    ones = jnp.ones((SUM_ROWS, ck), BF16)

    def pv(chunk, slot):
        for h in range(hps):
            vt = [v_ref[chunk * slabs + t, h * dv:(h + 1) * dv, :] for t in range(slabs)]
            vt = vt[0] if slabs == 1 else jnp.concatenate(vt, axis=1)
            vt = jnp.concatenate([vt, ones], axis=0)
            for c in range(n_comp):
                st = h * n_comp + c
                acc_ref[st] = acc_ref[st] * a_ref[slot * ns + st] + _dot(vt, p_ref[slot * ns + st])

    n = (i * tq) // ck + 1
    qk(0, 0)

    @pl.when(n >= 2)
    def _():
        sm(0, 0, False)
        qk(1, 1)

    def pair_body(u, carry):
        c0 = 2 * u
        qk(c0 + 2, 0)
        pv(c0, 0)
        sm(c0 + 1, 1, False)
        qk(c0 + 3, 1)
        pv(c0 + 1, 1)
        sm(c0 + 2, 0, False)
        return carry

    pairs = jnp.maximum(n - 2, 0) // 2
    lax.fori_loop(0, pairs, pair_body, 0)

    @pl.when((n % 2 == 1) & (n >= 3))
    def _():
        qk(n - 1, 0)
        pv(n - 3, 0)
        sm(n - 2, 1, False)
        pv(n - 2, 1)

    @pl.when(n % 2 == 1)
    def _():
        sm(n - 1, 0, True)
        pv(n - 1, 0)

    @pl.when(n % 2 == 0)
    def _():
        pv(n - 2, 0)
        sm(n - 1, 1, True)
        pv(n - 1, 1)

    for h in range(hps):
        outs = [acc_ref[h * n_comp + c, 0:dv, :] / acc_ref[h * n_comp + c, dv:dv + 1, :]
                for c in range(n_comp)]
        if diff_lambda_init is None:
            o = outs[0]
        else:
            lam = (jnp.exp(jnp.sum(lq1_ref[...] * lk1_ref[...], axis=1, keepdims=True))
                   - jnp.exp(jnp.sum(lq2_ref[...] * lk2_ref[...], axis=1, keepdims=True))
                   + diff_lambda_init)
            o = outs[0] - lam * outs[1]
            ms = jnp.mean(o * o, axis=0, keepdims=True)
            o = o * lax.rsqrt(ms + NORM_EPS) * _rep(gs_ref[...], tq // LANES) * (1.0 - diff_lambda_init)
        o_ref[:, h * dv:(h + 1) * dv] = o.T.astype(o_ref.dtype)


def _flash_t(q_t, k, v_t, *, batch, seq, heads, tq, ck, dk, dv, name, hps=1, n_comp=1, kmean=None, diff=None):
    assert seq % ck == 0 and ck % tq == 0 and tq % VT_CHUNK == 0 and heads % hps == 0
    nq = seq // tq
    dq = q_t.shape[1] // heads
    assert q_t.shape == (batch * nq, heads * dq, tq) and k.shape == (batch * seq, heads * dk)
    assert v_t.shape == (batch * seq // VT_CHUNK, heads * dv, VT_CHUNK)
    slabs_per_seq = seq // VT_CHUNK
    in_specs = [
        pl.BlockSpec((1, hps * dq, tq), lambda b, h, i: (b * nq + i, h, 0)),
        pl.BlockSpec((seq, hps * dk), lambda b, h, i: (b, h)),
        pl.BlockSpec((slabs_per_seq, hps * dv, VT_CHUNK), lambda b, h, i: (b, h, 0)),
    ]
    args = [q_t, k, v_t]
    moba_topk = 0
    if kmean is not None:
        nb = seq // MOBA_BLOCK
        assert tq % MOBA_BLOCK == 0 and nb <= LANES and dq + LANES == dk
        moba_topk = min(MOBA_TOPK, nb)
        in_specs.append(pl.BlockSpec((nb, hps * dq), lambda b, h, i: (b, h)))
        args.append(kmean)
    lambda_init = None
    if diff is not None:
        lq1, lk1, lq2, lk2, g_sub, lambda_init = diff
        for a in (lq1, lk1, lq2, lk2):
            in_specs.append(pl.BlockSpec((1, a.shape[0]), lambda b, h, i: (0, 0)))
            args.append(a.reshape(1, -1).astype(F32))
        in_specs.append(pl.BlockSpec((dv, LANES), lambda b, h, i: (0, 0)))
        args.append(jnp.broadcast_to(g_sub.astype(F32)[:, None], (dv, LANES)))
    kern = functools.partial(_flash_t_kernel, tq=tq, ck=ck, hps=hps, n_comp=n_comp, comp_dk=dk // n_comp,
                             moba_topk=moba_topk, diff_lambda_init=lambda_init)
    ns = hps * n_comp
    scratch = [
        pltpu.VMEM((2 * ns, ck, tq), F32),
        pltpu.VMEM((2 * ns, ck, tq), BF16),
        pltpu.VMEM((2 * ns, 1, tq), F32),
        pltpu.VMEM((ns, 1, tq), F32),
        pltpu.VMEM((ns, dv + SUM_ROWS, tq), F32),
    ]
    return pl.pallas_call(
        kern,
        grid=(batch, heads // hps, nq),
        in_specs=in_specs,
        out_specs=pl.BlockSpec((tq, hps * dv), lambda b, h, i: (b * nq + i, h)),
        out_shape=jax.ShapeDtypeStruct((batch * seq, heads * dv), BF16),
        scratch_shapes=scratch,
        compiler_params=_cparams("parallel", "parallel", "arbitrary"),
        name=name,
    )(*args)


def _swa_kernel(sink_ref, q_ref, k_ref, v_ref, o_ref, *, tq, window, group):
    kh = pl.program_id(1)
    i = pl.program_id(2)
    half = LANES // 2
    n_sub = tq // window
    pairs = group // 2

    def sub(sb, carry):
        r0 = i * tq + sb * window
        k0 = jnp.maximum(r0 - window, 0)
        qrow = sb * window
        kstart = pl.multiple_of(k0, window)
        kk = k_ref[pl.ds(kstart, 2 * window), :]
        vv = v_ref[pl.ds(kstart, 2 * window), :]
        qpos = r0 + lax.broadcasted_iota(jnp.int32, (window, 2 * window), 0)
        kpos = k0 + lax.broadcasted_iota(jnp.int32, (window, 2 * window), 1)
        valid = (kpos <= qpos) & (kpos > qpos - window)
        lane_lo = lax.broadcasted_iota(jnp.int32, (window, LANES), 1) < half
        for pr in range(pairs):
            q2 = q_ref[pl.ds(qrow, window), pr * LANES:(pr + 1) * LANES]
            outs = []
            for e in range(2):
                keep = lane_lo if e == 0 else jnp.logical_not(lane_lo)
                qe = jnp.where(keep, q2, jnp.zeros_like(q2))
                s = jnp.where(valid, _dot_t(qe, kk), NEG_INF)
                sink = sink_ref[kh * group + 2 * pr + e]
                m = jnp.maximum(jnp.max(s, axis=1, keepdims=True), sink)
                p = jnp.exp(s - m)
                denom = jnp.sum(p, axis=1, keepdims=True) + jnp.exp(sink - m)
                p = p / denom
                outs.append(_dot(p.astype(BF16), vv))
            o_pair = jnp.where(lane_lo, outs[0], outs[1])
            o_ref[pl.ds(qrow, window), pr * LANES:(pr + 1) * LANES] = o_pair.astype(o_ref.dtype)
        return carry

    for sb in range(n_sub):
        sub(sb, 0)


def _swa_attention(q, k, v, sinks, *, batch, seq, kv_heads, tq=512):
    tq = min(tq, seq)
    assert seq % tq == 0 and tq % SWA_WINDOW == 0 and seq >= 2 * SWA_WINDOW
    nq = seq // tq
    gw = SWA_GROUP * SWA_HEAD_DIM
    kern = functools.partial(_swa_kernel, tq=tq, window=SWA_WINDOW, group=SWA_GROUP)
    return pl.pallas_call(
        kern,
        grid=(batch, kv_heads, nq),
        in_specs=[
            pl.BlockSpec(memory_space=pltpu.SMEM),
            pl.BlockSpec((tq, gw), lambda b, h, i: (b * nq + i, h)),
            pl.BlockSpec((seq, LANES), lambda b, h, i: (b, h)),
            pl.BlockSpec((seq, LANES), lambda b, h, i: (b, h)),
        ],
        out_specs=pl.BlockSpec((tq, gw), lambda b, h, i: (b * nq + i, h)),
        out_shape=jax.ShapeDtypeStruct((batch * seq, kv_heads * gw), BF16),
        compiler_params=_cparams("parallel", "parallel", "arbitrary"),
        name="swa_attn",
    )(sinks.astype(F32), q, k, v)


def _tile_gain(g, n):
    return jnp.tile(g.astype(F32), n)


def _moba_layer(x, an, wq, wk, wv, gq, gk, wo, *, batch, seq):
    dh, blk = MOBA_HEAD_DIM, MOBA_BLOCK
    heads = wq.shape[1] // dh
    rot = dh // ROT_FRACTION
    rope = _rope_tables(seq, LANES, rot)
    tiles_per_seq = seq // 512
    tq = min(MOBA_QUERY_TILE, seq)
    q_t = _norm_proj_t(x, an, wq.T.astype(BF16), pattern=[("norm", dh, True)], seq=seq, pos_chunk=tq,
                       hg=_tile_gain(gq, heads), rope=rope, rope_half=rot // 2, out_scale=dh ** -0.5 * LOG2E,
                       name="moba_q_proj")
    block_id = jnp.arange(seq, dtype=jnp.int32)[:, None] // blk
    onehot = (block_id == jnp.arange(LANES, dtype=jnp.int32)[None, :]).astype(F32)
    k, kmean = _norm_proj(x, an, wk.astype(BF16), pattern=[("norm", dh, False, True)], seq=seq,
                          hg=_tile_gain(gk, heads), rope=rope, rope_half=rot // 2, kmean_rows=blk,
                          pair=(onehot, lambda i, j: (i % tiles_per_seq, 0)), name="moba_k_proj")
    v_t = _norm_proj_t(x, an, wv.T.astype(BF16), pattern=[("copy",)], seq=seq, pos_chunk=VT_CHUNK,
                       name="moba_v_proj")
    kmean = kmean.reshape(kmean.shape[0], kmean.shape[2])
    o = _flash_t(q_t, k, v_t, batch=batch, seq=seq, heads=heads, tq=tq, ck=min(MOBA_KEY_CHUNK, seq), dk=2 * dh,
                 dv=dh, hps=ATTN_HEADS_PER_STEP, kmean=kmean, name="moba_attn")
    return _matmul_residual(o, wo.astype(BF16), x, name="moba_out_proj")


def _mla_layer(x, an, wq_a, g_qa, wq_b, wkv_a, g_kva, wkv_b, g_qn, g_kn, g_qr, g_kr, wo, *, batch, seq):
    d = x.shape[1]
    nope, rd, vd, pad = MLA_NOPE_DIM, MLA_ROPE_DIM, MLA_V_DIM, MLA_QK_PAD
    q_rank = wq_a.shape[1]
    kv_rank = wkv_a.shape[1] - rd
    heads = wq_b.shape[1] // (nope + rd)
    rope = _rope_tables(seq, LANES, rd)
    zpad = lambda n: jnp.zeros((n,), F32)
    tq = min(512, seq)

    w_a = jnp.concatenate([wq_a, wkv_a, jnp.zeros((d, LANES - rd), F32)], axis=1).astype(BF16)
    n_a = w_a.shape[1]
    chunks_a = n_a // LANES
    hg_a = jnp.concatenate([jnp.ones((q_rank + kv_rank,), F32), g_kr.astype(F32), zpad(LANES - rd)])
    pat_a = [("copy",)] * (chunks_a - 1) + [("norm", rd, False, True)]
    lat = _norm_proj(x, an, w_a, pattern=pat_a, seq=seq, hg=hg_a, rope=rope, rope_half=rd // 2,
                     out_dtype=F32, name="mla_latent_proj")

    wq_b3 = wq_b.reshape(q_rank, heads, nope + rd)
    wq_p = jnp.concatenate([wq_b3, jnp.zeros((q_rank, heads, pad - nope - rd), F32)], axis=2)
    wq_p = wq_p.reshape(q_rank, heads * pad)
    hg_q = jnp.tile(jnp.concatenate([g_qn.astype(F32), g_qr.astype(F32), zpad(pad - nope - rd)]), heads)
    pat_q = [("norm", nope, False), ("norm", rd, True)]
    q_t = _norm_proj_t(lat, g_qa, wq_p.T.astype(BF16), pattern=pat_q, seq=seq, pos_chunk=tq, xcol=0, hg=hg_q,
                       rope=rope, rope_half=rd // 2, out_scale=(nope + rd) ** -0.5 * LOG2E, name="mla_q_proj")

    wkv3 = wkv_b.reshape(kv_rank, heads, nope + vd)
    w_k = wkv3[:, :, :nope].reshape(kv_rank, heads * nope).astype(BF16)
    w_v = wkv3[:, :, nope:].reshape(kv_rank, heads * vd)
    assert q_rank == kv_rank
    kr_col = (q_rank + kv_rank) // LANES
    k = _norm_proj(lat, g_kva, w_k, pattern=[("norm", nope, False, False)], seq=seq, xcol=1,
                   hg=_tile_gain(g_kn, heads), pair=(lat, lambda i, j: (i, kr_col)), name="mla_k_proj")
    v_t = _norm_proj_t(lat, g_kva, w_v.T.astype(BF16), pattern=[("copy",)], seq=seq, pos_chunk=VT_CHUNK, xcol=1,
                       name="mla_v_proj")
    o = _flash_t(q_t, k, v_t, batch=batch, seq=seq, heads=heads, tq=tq, ck=tq, dk=pad, dv=vd,
                 hps=ATTN_HEADS_PER_STEP, name="mla_attn")
    return _matmul_residual(o, wo.astype(BF16), x, name="mla_out_proj")


def _swa_layer(x, an, wq, wk, wv, gq, gk, sinks, wo, *, batch, seq):
    d = x.shape[1]
    dh = SWA_HEAD_DIM
    hq = wq.shape[1] // dh
    hkv = wk.shape[1] // dh
    rot = dh // ROT_FRACTION
    rope = _rope_tables(seq, dh, rot)

    def dup(w):
        w3 = w.reshape(d, hkv, dh)
        return jnp.concatenate([w3, w3], axis=2).reshape(d, hkv * 2 * dh).astype(BF16)

    q = _norm_proj(x, an, wq.astype(BF16), pattern=[("norm", dh, True, True)], seq=seq, hg=_tile_gain(gq, hq),
                   rope=rope, rope_half=rot // 2, out_scale=dh ** -0.5, name="swa_q_proj")
    k = _norm_proj(x, an, dup(wk), pattern=[("norm", 2 * dh, False, True)], seq=seq,
                   hg=_tile_gain(gk, 2 * hkv), rope=rope, rope_half=rot // 2, name="swa_k_proj")
    v = _norm_proj(x, an, dup(wv), pattern=[("copy",)], seq=seq, name="swa_v_proj")
    o = _swa_attention(q, k, v, sinks, batch=batch, seq=seq, kv_heads=hkv)
    return _matmul_residual(o, wo.astype(BF16), x, name="swa_out_proj")


def _diff_layer(x, an, wq, wk, wv, gq, gk, lq1, lk1, lq2, lk2, g_sub, wo, lambda_init, *, batch, seq):
    dh = DIFF_HEAD_DIM
    heads = wq.shape[1] // (2 * dh)
    rot = dh // ROT_FRACTION
    rope = _rope_tables(seq, LANES, rot)
    tq = min(512, seq)
    q_t = _norm_proj_t(x, an, wq.T.astype(BF16), pattern=[("norm", dh, True)], seq=seq, pos_chunk=tq,
                       hg=_tile_gain(gq, 2 * heads), rope=rope, rope_half=rot // 2, out_scale=dh ** -0.5 * LOG2E,
                       name="diff_q_proj")
    k = _norm_proj(x, an, wk.astype(BF16), pattern=[("norm", dh, False, True)], seq=seq,
                   hg=_tile_gain(gk, 2 * heads), rope=rope, rope_half=rot // 2, name="diff_k_proj")
    v_t = _norm_proj_t(x, an, wv.T.astype(BF16), pattern=[("copy",)], seq=seq, pos_chunk=VT_CHUNK,
                       name="diff_v_proj")
    o = _flash_t(q_t, k, v_t, batch=batch, seq=seq, heads=heads, tq=tq, ck=tq, dk=2 * dh, dv=2 * dh, n_comp=2,
                 diff=(lq1, lk1, lq2, lk2, g_sub, lambda_init), name="diff_attn")
    return _matmul_residual(o, wo.astype(BF16), x, name="diff_out_proj")


def _ffn_layer(x, fn, w_gate, w_up, conv_w, conv_b, w_down, *, seq):
    act = _ffn_up(x, fn, w_gate.astype(BF16), w_up.astype(BF16), conv_w, conv_b, seq=seq)
    return _matmul_residual(act, w_down.astype(BF16), x, name="ffn_down")


def kernel(x, attn_norm, ffn_norm, moba_wq, moba_wk, moba_wv, moba_gq, moba_gk, moba_wo, mla_wq_a, mla_g_qa, mla_wq_b, mla_wkv_a, mla_g_kva, mla_wkv_b, mla_g_qn, mla_g_kn, mla_g_qr, mla_g_kr, mla_wo, swa_wq, swa_wk, swa_wv, swa_gq, swa_gk, swa_sinks, swa_wo, diff_wq, diff_wk, diff_wv, diff_gq, diff_gk, diff_lq1, diff_lk1, diff_lq2, diff_lk2, diff_g_sub, diff_wo, ffn_w_gate, ffn_w_up, ffn_conv_w, ffn_conv_b, ffn_w_down):
    batch, seq, d = x.shape
    h = x.reshape(batch * seq, d)
    kw = dict(batch=batch, seq=seq)
    for i in range(DEPTH):
        m, j = i % 4, i // 4
        if m == 0:
            h = _moba_layer(h, attn_norm[i], moba_wq[j], moba_wk[j], moba_wv[j], moba_gq[j], moba_gk[j],
                            moba_wo[j], **kw)
        elif m == 1:
            h = _mla_layer(h, attn_norm[i], mla_wq_a[j], mla_g_qa[j], mla_wq_b[j], mla_wkv_a[j], mla_g_kva[j],
                           mla_wkv_b[j], mla_g_qn[j], mla_g_kn[j], mla_g_qr[j], mla_g_kr[j], mla_wo[j], **kw)
        elif m == 2:
            h = _swa_layer(h, attn_norm[i], swa_wq[j], swa_wk[j], swa_wv[j], swa_gq[j], swa_gk[j],
                           swa_sinks[j], swa_wo[j], **kw)
        else:
            lambda_init = 0.8 - 0.6 * math.exp(-0.3 * i)
            h = _diff_layer(h, attn_norm[i], diff_wq[j], diff_wk[j], diff_wv[j], diff_gq[j], diff_gk[j],
                            diff_lq1[j], diff_lk1[j], diff_lq2[j], diff_lk2[j], diff_g_sub[j], diff_wo[j],
                            lambda_init, **kw)
        h = _ffn_layer(h, ffn_norm[i], ffn_w_gate[i], ffn_w_up[i], ffn_conv_w[i], ffn_conv_b[i],
                       ffn_w_down[i], seq=seq)
    return h.reshape(batch, seq, d)
```
